```python
import jax, jax.numpy as jnp
from jax import lax
import numpy as np

D_MODEL = 1024
BATCH = 8
SEQ = 4096
DEPTH = 2

GRID_W = 64
NA_HEADS = 8
NA_HEAD_DIM = 64
NA_WIN_ROWS = 8
NA_WIN_COLS = 16
NA_WIDTH = NA_HEADS * NA_HEAD_DIM
MLA_HEADS = 8
MLA_Q_RANK = 384
MLA_KV_RANK = 256
MLA_NOPE_DIM = 64
MLA_ROPE_DIM = 32
MLA_V_DIM = 64
MLA_WIDTH = MLA_HEADS * MLA_V_DIM
ROPE_BASE = 10000.0
Q_BLOCK = 128
N_EXPERTS = 16
EC_CAPACITY = 2
EXPERT_FF = 1024
PLE_DIM = 256
RMS_EPS = 1e-6
IN_SPLITS = [NA_WIDTH, NA_WIDTH, NA_WIDTH, MLA_Q_RANK, MLA_KV_RANK, MLA_ROPE_DIM, D_MODEL, D_MODEL]
IN_COLS = sum(IN_SPLITS)
IN_CUTS = [int(c) for c in np.cumsum(IN_SPLITS)[:-1]]

kernel_name = "hybrid_na_mla_ec_moe_encoder"


def rmsnorm(x, g):
    xf = x.astype(jnp.float32)
    y = xf * lax.rsqrt(jnp.mean(xf * xf, axis=-1, keepdims=True) + RMS_EPS)
    return (y * g.astype(jnp.float32)).astype(x.dtype)


def rope(x, pos):
    dim = x.shape[-1]
    half = dim // 2
    freqs = 1.0 / (ROPE_BASE ** (jnp.arange(0, dim, 2, dtype=jnp.float32) / dim))
    ang = pos.astype(jnp.float32)[:, None] * freqs[None, :]
    shape = (ang.shape[0],) + (1,) * (x.ndim - 3) + (half,)
    cos = jnp.cos(ang).reshape(shape).astype(x.dtype)
    sin = jnp.sin(ang).reshape(shape).astype(x.dtype)
    x1, x2 = x[..., :half], x[..., half:]
    return jnp.concatenate([x1 * cos - x2 * sin, x2 * cos + x1 * sin], axis=-1)


def rope2d(x, rows_pos, cols_pos):
    half = x.shape[-1] // 2
    return jnp.concatenate([rope(x[..., :half], rows_pos), rope(x[..., half:], cols_pos)], axis=-1)


def neighbourhood_attention(q, k, v, rpb):
    B, S, H, d = q.shape
    rows = S // GRID_W
    kh = min(NA_WIN_ROWS, rows)
    kw = NA_WIN_COLS
    scale = d ** -0.5
    qg = q.reshape(B, rows, GRID_W, H, d).transpose(1, 0, 2, 3, 4)
    kg = k.reshape(B, rows, GRID_W, H, d)
    vg = v.reshape(B, rows, GRID_W, H, d)
    col = jnp.arange(GRID_W)
    col_start = jnp.clip(col - kw // 2, 0, GRID_W - kw)
    col_idx = col_start[:, None] + jnp.arange(kw)[None, :]
    col_bias_idx = col_idx - col[:, None] + (NA_WIN_COLS - 1)

    def row_block(args):
        r, q_row = args
        r0 = jnp.clip(r - kh // 2, 0, rows - kh)
        k_rows = lax.dynamic_slice_in_dim(kg, r0, kh, axis=1)
        v_rows = lax.dynamic_slice_in_dim(vg, r0, kh, axis=1)
        k_win = k_rows[:, :, col_idx]
        v_win = v_rows[:, :, col_idx]
        row_bias_idx = r0 + jnp.arange(kh) - r + (NA_WIN_ROWS - 1)
        bias = rpb[:, row_bias_idx][:, :, col_bias_idx]
        bias = bias.transpose(0, 2, 1, 3).astype(jnp.float32)
        s = jnp.einsum('bqhd,bkqjhd->bhqkj', q_row, k_win).astype(jnp.float32) * scale + bias[None]
        pr = jax.nn.softmax(s.reshape(B, H, GRID_W, kh * kw), axis=-1)
        pr = pr.reshape(B, H, GRID_W, kh, kw).astype(v.dtype)
        return jnp.einsum('bhqkj,bkqjhd->bqhd', pr, v_win)

    out = lax.map(row_block, (jnp.arange(rows), qg))
    return out.transpose(1, 0, 2, 3, 4).reshape(B, S, H * d)


def latent_attention(q_lat, kv_lat, k_rope, q_norm, wq_up, kv_norm, wkv_up, rows_pos, cols_pos):
    B, S, _ = q_lat.shape
    H = MLA_HEADS
    q = (rmsnorm(q_lat, q_norm) @ wq_up).reshape(B, S, H, MLA_NOPE_DIM + MLA_ROPE_DIM)
    q_nope, q_pe = q[..., :MLA_NOPE_DIM], q[..., MLA_NOPE_DIM:]
    q_pe = rope2d(q_pe, rows_pos, cols_pos)
    kv = (rmsnorm(kv_lat, kv_norm) @ wkv_up).reshape(B, S, H, MLA_NOPE_DIM + MLA_V_DIM)
    k_nope, v = kv[..., :MLA_NOPE_DIM], kv[..., MLA_NOPE_DIM:]
    k_pe = rope2d(k_rope, rows_pos, cols_pos)
    k = jnp.concatenate([k_nope, jnp.broadcast_to(k_pe[:, :, None, :], (B, S, H, MLA_ROPE_DIM))], axis=-1)
    qf = jnp.concatenate([q_nope, q_pe], axis=-1)
    dqk = MLA_NOPE_DIM + MLA_ROPE_DIM
    scale = dqk ** -0.5
    nb = S // Q_BLOCK
    qb = qf.reshape(B, nb, Q_BLOCK, H, dqk).transpose(1, 0, 2, 3, 4)

    def block(qblk):
        s = jnp.einsum('bqhd,bkhd->bhqk', qblk, k).astype(jnp.float32) * scale
        pr = jax.nn.softmax(s, axis=-1).astype(v.dtype)
        return jnp.einsum('bhqk,bkhv->bqhv', pr, v)

    out = lax.map(block, qb)
    return out.transpose(1, 0, 2, 3, 4).reshape(B, S, H * MLA_V_DIM)


def expert_choice_moe(h, w_router, w1, w3, w2):
    B, S, D = h.shape
    cap = EC_CAPACITY * S // N_EXPERTS
    logits = jnp.einsum('bsd,de->bse', h, w_router).astype(jnp.float32)
    aff = jax.nn.softmax(logits, axis=-1)
    vals, idx = lax.top_k(aff.transpose(0, 2, 1), cap)
    bidx = jnp.arange(B)[:, None, None]
    xg = h[bidx, idx]
    h1 = jnp.einsum('becd,edf->becf', xg, w1)
    h3 = jnp.einsum('becd,edf->becf', xg, w3)
    out = jnp.einsum('becf,efd->becd', jax.nn.silu(h1) * h3, w2)
    out = out * vals.astype(h.dtype)[..., None]
    return jnp.zeros_like(h).at[bidx, idx].add(out)


def setup_inputs(seed: int = 0) -> dict:
    key = jax.random.key(seed)
    ks = jax.random.split(key, 24)

    def nrm(k, shape, scale):
        return jax.random.normal(k, shape, jnp.float32) * scale

    def gain(k, shape):
        return 1.0 + 0.01 * jax.random.normal(k, shape, jnp.float32)

    L, D = DEPTH, D_MODEL
    return {
        "x": nrm(ks[0], (BATCH, SEQ, D), 1.0),
        "p": nrm(ks[1], (DEPTH, BATCH, SEQ, PLE_DIM), 1.0),
        "norm_mix": gain(ks[2], (L, D)),
        "w_in": nrm(ks[3], (L, D, IN_COLS), D ** -0.5),
        "na_rpb": nrm(ks[4], (L, NA_HEADS, 2 * NA_WIN_ROWS - 1, 2 * NA_WIN_COLS - 1), 0.02),
        "mla_q_norm": gain(ks[5], (L, MLA_Q_RANK)),
        "mla_wq_up": nrm(ks[6], (L, MLA_Q_RANK, MLA_HEADS * (MLA_NOPE_DIM + MLA_ROPE_DIM)), MLA_Q_RANK ** -0.5),
        "mla_kv_norm": gain(ks[7], (L, MLA_KV_RANK)),
        "mla_wkv_up": nrm(ks[8], (L, MLA_KV_RANK, MLA_HEADS * (MLA_NOPE_DIM + MLA_V_DIM)), MLA_KV_RANK ** -0.5),
        "w_na_o": nrm(ks[9], (L, NA_WIDTH, D), NA_WIDTH ** -0.5),
        "w_mla_o": nrm(ks[10], (L, MLA_WIDTH, D), MLA_WIDTH ** -0.5),
        "w_out": nrm(ks[11], (L, D, D), D ** -0.5),
        "norm_moe": gain(ks[12], (L, D)),
        "w_router": nrm(ks[13], (L, D, N_EXPERTS), D ** -0.5),
        "moe_w1": nrm(ks[14], (L, N_EXPERTS, D, EXPERT_FF), D ** -0.5),
        "moe_w3": nrm(ks[15], (L, N_EXPERTS, D, EXPERT_FF), D ** -0.5),
        "moe_w2": nrm(ks[16], (L, N_EXPERTS, EXPERT_FF, D), EXPERT_FF ** -0.5),
        "norm_ple": gain(ks[17], (L, D)),
        "ple_gate_w": nrm(ks[18], (L, D, D), D ** -0.5),
        "ple_w": nrm(ks[19], (L, PLE_DIM, D), PLE_DIM ** -0.5),
        "norm_final": gain(ks[20], (D,)),
    }


def reference(x, p, norm_mix, w_in, na_rpb, mla_q_norm, mla_wq_up, mla_kv_norm, mla_wkv_up,
              w_na_o, w_mla_o, w_out, norm_moe, w_router, moe_w1, moe_w3, moe_w2,
              norm_ple, ple_gate_w, ple_w, norm_final):
    B, S, D = x.shape
    t = jnp.arange(S)
    rows_pos = t // GRID_W
    cols_pos = t % GRID_W
    for i in range(DEPTH):
        h = rmsnorm(x, norm_mix[i])
        proj = h @ w_in[i]
        na_q, na_k, na_v, q_lat, kv_lat, k_rope, gate_a, gate_b = jnp.split(proj, IN_CUTS, axis=-1)
        y_a = neighbourhood_attention(na_q.reshape(B, S, NA_HEADS, NA_HEAD_DIM),
                                      na_k.reshape(B, S, NA_HEADS, NA_HEAD_DIM),
                                      na_v.reshape(B, S, NA_HEADS, NA_HEAD_DIM),
                                      na_rpb[i]) @ w_na_o[i]
        y_b = latent_attention(q_lat, kv_lat, k_rope, mla_q_norm[i], mla_wq_up[i],
                               mla_kv_norm[i], mla_wkv_up[i], rows_pos, cols_pos) @ w_mla_o[i]
        merged = jax.nn.sigmoid(gate_a) * y_a + jax.nn.sigmoid(gate_b) * y_b
        x = x + merged @ w_out[i]
        x = x + expert_choice_moe(rmsnorm(x, norm_moe[i]), w_router[i], moe_w1[i], moe_w3[i], moe_w2[i])
        ple_gate = jax.nn.sigmoid(rmsnorm(x, norm_ple[i]) @ ple_gate_w[i])
        x = x + ple_gate * (p[i] @ ple_w[i])
    return rmsnorm(x, norm_final)
```

```python
import functools
import math

import jax
import jax.numpy as jnp
import numpy as np
from jax import lax
from jax.experimental import pallas as pl
from jax.experimental.pallas import tpu as pltpu

F32 = jnp.float32
BF16 = jnp.bfloat16

D_MODEL = 1024
GRID_W = 64
NA_HEADS = 8
NA_HEAD_DIM = 64
NA_WIN_ROWS = 8
NA_WIN_COLS = 16
NA_WIDTH = NA_HEADS * NA_HEAD_DIM
MLA_HEADS = 8
MLA_Q_RANK = 384
MLA_KV_RANK = 256
MLA_NOPE = 64
MLA_ROPE = 32
MLA_V = 64
MLA_WIDTH = MLA_HEADS * MLA_V
ROPE_BASE = 10000.0
N_EXPERTS = 16
EC_CAPACITY = 2
EXPERT_FF = 1024
PLE_DIM = 256
RMS_EPS = 1e-6

LANES = 128
SUBLANES = 8
HEAD_SLOT = 128
VMEM_LIMIT = 56 * 1024 * 1024

C_GATE_A = 0
C_GATE_B = 1024
C_NA_Q = 2048
C_NA_K = 2560
C_NA_V = 3072
C_Q_LAT = 3584
Q_LAT_PAD = 512
C_KV_LAT = 4096
C_K_ROPE = 4352
N_PROJ = 4608

TM = 512
NA_ROWS = 4
NA_KROWS = 12
FLASH_TQ = 256
FLASH_TK = 512
MOE_TF = 512
TILE_PITCH = 520
GATHER_UNROLL = 8
NEG_BIG = -1e30
SELECT_COLS = 16

MLA_QSCALE = float((MLA_NOPE + MLA_ROPE) ** -0.5 * math.log2(math.e))


def _dot(a, b):
    return jnp.dot(a, b, preferred_element_type=F32)


def _dot_nt(a, b):
    return lax.dot_general(a, b, (((1,), (1,)), ((), ())), preferred_element_type=F32)


def _rms(x, g, n=None):
    n = x.shape[-1] if n is None else n
    ms = jnp.sum(x * x, axis=-1, keepdims=True) * (1.0 / n)
    return (x * lax.rsqrt(ms + RMS_EPS)) * g


def _sigmoid(x):
    return 1.0 / (1.0 + jnp.exp(-x))


def _params(sem, vmem=VMEM_LIMIT):
    return pltpu.CompilerParams(dimension_semantics=sem, vmem_limit_bytes=vmem)


def _in_proj_kernel(x_ref, g_ref, w_ref, o_ref):
    h = _rms(x_ref[...], g_ref[...]).astype(BF16)
    for n in range(N_PROJ // 256):
        sl = slice(n * 256, (n + 1) * 256)
        o_ref[:, sl] = _dot(h, w_ref[:, sl]).astype(BF16)


def _in_proj(x2, g, w):
    t = x2.shape[0]
    return pl.pallas_call(
        _in_proj_kernel,
        grid=(t // TM,),
        in_specs=[
            pl.BlockSpec((TM, D_MODEL), lambda i: (i, 0)),
            pl.BlockSpec((1, D_MODEL), lambda i: (0, 0)),
            pl.BlockSpec((D_MODEL, N_PROJ), lambda i: (0, 0)),
        ],
        out_specs=pl.BlockSpec((TM, N_PROJ), lambda i: (i, 0)),
        out_shape=jax.ShapeDtypeStruct((t, N_PROJ), BF16),
        compiler_params=_params(("parallel",)),
        name="in_proj",
    )(x2, g, w)


def _na_kernel(q_ref, k_ref, v_ref, b_ref, o_ref):
    m = pl.program_id(2)
    rows = k_ref.shape[0] // GRID_W
    kr0 = jnp.clip(NA_ROWS * m - NA_ROWS, 0, rows - NA_KROWS)
    start = pl.multiple_of(kr0 * GRID_W, NA_ROWS * GRID_W)
    nk = NA_KROWS * GRID_W
    k = k_ref[pl.ds(start, nk), :]
    v = v_ref[pl.ds(start, nk), :]
    q = q_ref[...]
    lane = lax.broadcasted_iota(jnp.int32, q.shape, 1)
    scale = NA_HEAD_DIM ** -0.5
    outs = []
    for h in range(2):
        in_head = (lane >= h * NA_HEAD_DIM) & (lane < (h + 1) * NA_HEAD_DIM)
        qh = jnp.where(in_head, q, jnp.zeros_like(q)) * scale
        s = _dot_nt(qh, k) + b_ref[0, h]
        mx = jnp.max(s, axis=1, keepdims=True)
        p = jnp.exp(s - mx)
        l = jnp.sum(p, axis=1, keepdims=True)
        outs.append(_dot(p.astype(BF16), v) / l)
    o_ref[...] = jnp.where(lane < NA_HEAD_DIM, outs[0], outs[1]).astype(BF16)


def _na_attention(proj, bias, batch, seq):
    rows = seq // GRID_W
    nblk = rows // NA_ROWS
    tq = NA_ROWS * GRID_W

    def variant(m):
        return jnp.where(m == 0, 0, jnp.where(m == nblk - 1, 2, 1))

    return pl.pallas_call(
        _na_kernel,
        grid=(batch, NA_HEADS // 2, nblk),
        in_specs=[
            pl.BlockSpec((tq, LANES), lambda b, hp, m: (b * nblk + m, C_NA_Q // LANES + hp)),
            pl.BlockSpec((seq, LANES), lambda b, hp, m: (b, C_NA_K // LANES + hp)),
            pl.BlockSpec((seq, LANES), lambda b, hp, m: (b, C_NA_V // LANES + hp)),
            pl.BlockSpec((1, 2, tq, NA_KROWS * GRID_W), lambda b, hp, m: (variant(m), hp, 0, 0)),
        ],
        out_specs=pl.BlockSpec((tq, LANES), lambda b, hp, m: (b * nblk + m, hp)),
        out_shape=jax.ShapeDtypeStruct((batch * seq, NA_WIDTH), BF16),
        compiler_params=_params(("parallel", "parallel", "arbitrary")),
        name="na_attention",
    )(proj, proj, proj, bias)


def _na_bias_tables(rpb, rows):
    nblk = rows // NA_ROWS
    qr = np.arange(NA_ROWS)
    kk = np.arange(NA_KROWS)
    row_sel = np.zeros((3, NA_ROWS, NA_KROWS, 2 * NA_WIN_ROWS - 1), np.float32)
    row_ok = np.zeros((3, NA_ROWS, NA_KROWS), bool)
    for vi, m in enumerate((0, 1, nblk - 1)):
        r = NA_ROWS * m + qr
        r0 = np.clip(r - NA_WIN_ROWS // 2, 0, rows - NA_WIN_ROWS)
        kr0 = np.clip(NA_ROWS * m - NA_ROWS, 0, rows - NA_KROWS)
        key_row = kr0 + kk
        ok = (key_row[None, :] >= r0[:, None]) & (key_row[None, :] < r0[:, None] + NA_WIN_ROWS)
        a = np.clip(key_row[None, :] - r[:, None] + NA_WIN_ROWS - 1, 0, 2 * NA_WIN_ROWS - 2)
        row_ok[vi] = ok
        row_sel[vi, qr[:, None], kk[None, :], a] = 1.0
    c = np.arange(GRID_W)
    cs = np.clip(c - NA_WIN_COLS // 2, 0, GRID_W - NA_WIN_COLS)
    col_ok = (c[None, :] >= cs[:, None]) & (c[None, :] < cs[:, None] + NA_WIN_COLS)
    bc = np.clip(c[None, :] - c[:, None] + NA_WIN_COLS - 1, 0, 2 * NA_WIN_COLS - 2)
    col_sel = np.zeros((GRID_W, GRID_W, 2 * NA_WIN_COLS - 1), np.float32)
    col_sel[c[:, None], c[None, :], bc] = 1.0
    bias = jnp.einsum("vqka,hab,cjb->vhqckj", jnp.asarray(row_sel), rpb.astype(F32), jnp.asarray(col_sel),
                      precision=lax.Precision.HIGHEST)
    ok = row_ok[:, None, :, None, :, None] & col_ok[None, None, None, :, None, :]
    bias = jnp.where(jnp.asarray(ok), bias, NEG_BIG)
    return bias.reshape(3, NA_HEADS, NA_ROWS * GRID_W, NA_KROWS * GRID_W)


def _mla_prep_kernel(ql_ref, kvl_ref, kr_ref, qn_ref, kvn_ref, wq_ref, wqr_ref, wk_ref, wv_ref,
                     ones_ref, cos_ref, sin_ref, q_out, k_out, v_out):
    cos = cos_ref[...]
    sin = sin_ref[...]
    cos8 = jnp.concatenate([cos] * MLA_HEADS, axis=1)
    sin8 = jnp.concatenate([sin] * MLA_HEADS, axis=1)
    hq = _rms(ql_ref[...].astype(F32), qn_ref[...], MLA_Q_RANK).astype(BF16)
    q = _dot(hq, wq_ref[...]) * cos8 + _dot(hq, wqr_ref[...]) * sin8
    q_out[...] = (q * MLA_QSCALE).astype(BF16)
    hkv = _rms(kvl_ref[...].astype(F32), kvn_ref[...]).astype(BF16)
    kr = kr_ref[...].astype(F32)
    kpe = kr[:, :HEAD_SLOT] * cos + kr[:, HEAD_SLOT:] * sin
    k = _dot(hkv, wk_ref[...]) + jnp.concatenate([kpe] * MLA_HEADS, axis=1)
    k_out[...] = k.astype(BF16)
    v_out[...] = (_dot(hkv, wv_ref[...]) + ones_ref[...]).astype(BF16)


def _mla_prep(proj, qn, kvn, wq, wqr, wk, wv, ones, cos_t, sin_t, seq):
    t = proj.shape[0]
    width = MLA_HEADS * HEAD_SLOT
    sblk = seq // TM
    full = lambda shape: pl.BlockSpec(shape, lambda i: (0, 0))
    out = jax.ShapeDtypeStruct((t, width), BF16)
    return pl.pallas_call(
        _mla_prep_kernel,
        grid=(t // TM,),
        in_specs=[
            pl.BlockSpec((TM, Q_LAT_PAD), lambda i: (i, C_Q_LAT // Q_LAT_PAD)),
            pl.BlockSpec((TM, MLA_KV_RANK), lambda i: (i, C_KV_LAT // MLA_KV_RANK)),
            pl.BlockSpec((TM, 2 * HEAD_SLOT), lambda i: (i, C_K_ROPE // (2 * HEAD_SLOT))),
            full((1, Q_LAT_PAD)), full((1, MLA_KV_RANK)),
            full((Q_LAT_PAD, width)), full((Q_LAT_PAD, width)),
            full((MLA_KV_RANK, width)), full((MLA_KV_RANK, width)),
            full((1, width)),
            pl.BlockSpec((TM, HEAD_SLOT), lambda i: (i % sblk, 0)),
            pl.BlockSpec((TM, HEAD_SLOT), lambda i: (i % sblk, 0)),
        ],
        out_specs=[pl.BlockSpec((TM, width), lambda i: (i, 0))] * 3,
        out_shape=[out, out, out],
        compiler_params=_params(("parallel",)),
        name="mla_prep",
    )(proj, proj, proj, qn, kvn, wq, wqr, wk, wv, ones, cos_t, sin_t)


def _flash_kernel(q_ref, k_ref, v_ref, o_ref):
    seq = k_ref.shape[0]
    outs = []
    for h in range(2):
        hs = slice(h * HEAD_SLOT, (h + 1) * HEAD_SLOT)
        q = q_ref[:, hs]

        def body(j, carry, hs=hs, q=q):
            m, acc = carry
            start = pl.multiple_of(j * FLASH_TK, FLASH_TK)
            k = k_ref[pl.ds(start, FLASH_TK), hs]
            v = v_ref[pl.ds(start, FLASH_TK), hs]
            s = _dot_nt(q, k)
            m_new = jnp.maximum(m, jnp.max(s, axis=1, keepdims=True))
            p = jnp.exp2(s - m_new)
            alpha = jnp.exp2(m - m_new)
            return m_new, alpha * acc + _dot(p.astype(BF16), v)

        m0 = jnp.full((FLASH_TQ, 1), NEG_BIG, F32)
        acc0 = jnp.zeros((FLASH_TQ, HEAD_SLOT), F32)
        _, acc = lax.fori_loop(0, seq // FLASH_TK, body, (m0, acc0))
        outs.append(acc / pltpu.roll(acc, MLA_V, axis=1))
    lane = lax.broadcasted_iota(jnp.int32, outs[0].shape, 1)
    o_ref[...] = jnp.where(lane < MLA_V, outs[0], outs[1]).astype(BF16)


def _flash(qp, kp, vp, batch, seq):
    nq = seq // FLASH_TQ
    return pl.pallas_call(
        _flash_kernel,
        grid=(batch, MLA_HEADS // 2, nq),
        in_specs=[
            pl.BlockSpec((FLASH_TQ, 2 * HEAD_SLOT), lambda b, hp, i: (b * nq + i, hp)),
            pl.BlockSpec((seq, 2 * HEAD_SLOT), lambda b, hp, i: (b, hp)),
            pl.BlockSpec((seq, 2 * HEAD_SLOT), lambda b, hp, i: (b, hp)),
        ],
        out_specs=pl.BlockSpec((FLASH_TQ, LANES), lambda b, hp, i: (b * nq + i, hp)),
        out_shape=jax.ShapeDtypeStruct((batch * seq, MLA_WIDTH), BF16),
        compiler_params=_params(("parallel", "parallel", "arbitrary")),
        name="mla_flash",
    )(qp, kp, vp)


def _merge_kernel(x_ref, ya_ref, yb_ref, ga_ref, gb_ref, wa_ref, wb_ref, wo_ref, o_ref):
    ya = _dot(ya_ref[...], wa_ref[...])
    yb = _dot(yb_ref[...], wb_ref[...])
    merged = _sigmoid(ga_ref[...].astype(F32)) * ya + _sigmoid(gb_ref[...].astype(F32)) * yb
    o_ref[...] = x_ref[...] + _dot(merged.astype(BF16), wo_ref[...])


def _merge(x2, ya, yb, proj, wa, wb, wo):
    t = x2.shape[0]
    full = lambda shape: pl.BlockSpec(shape, lambda i: (0, 0))
    return pl.pallas_call(
        _merge_kernel,
        grid=(t // TM,),
        in_specs=[
            pl.BlockSpec((TM, D_MODEL), lambda i: (i, 0)),
            pl.BlockSpec((TM, NA_WIDTH), lambda i: (i, 0)),
            pl.BlockSpec((TM, MLA_WIDTH), lambda i: (i, 0)),
            pl.BlockSpec((TM, D_MODEL), lambda i: (i, C_GATE_A // D_MODEL)),
            pl.BlockSpec((TM, D_MODEL), lambda i: (i, C_GATE_B // D_MODEL)),
            full((NA_WIDTH, D_MODEL)), full((MLA_WIDTH, D_MODEL)), full((D_MODEL, D_MODEL)),
        ],
        out_specs=pl.BlockSpec((TM, D_MODEL), lambda i: (i, 0)),
        out_shape=jax.ShapeDtypeStruct((t, D_MODEL), F32),
        compiler_params=_params(("parallel",)),
        name="merge",
    )(x2, ya, yb, proj, proj, wa, wb, wo)


def _router_kernel(x_ref, g_ref, wt_ref, o_ref):
    h = _rms(x_ref[...], g_ref[...])
    logits = lax.dot_general(wt_ref[...], h, (((1,), (1,)), ((), ())),
                             precision=lax.Precision.HIGHEST, preferred_element_type=F32)
    mx = jnp.max(logits, axis=0, keepdims=True)
    e = jnp.exp(logits - mx)
    o_ref[0] = e / jnp.sum(e, axis=0, keepdims=True)


def _router(x2, g, wt, batch, seq):
    sblk = seq // TM
    return pl.pallas_call(
        _router_kernel,
        grid=(batch, sblk),
        in_specs=[
            pl.BlockSpec((TM, D_MODEL), lambda b, i: (b * sblk + i, 0)),
            pl.BlockSpec((1, D_MODEL), lambda b, i: (0, 0)),
            pl.BlockSpec((N_EXPERTS, D_MODEL), lambda b, i: (0, 0)),
        ],
        out_specs=pl.BlockSpec((1, N_EXPERTS, TM), lambda b, i: (b, 0, i)),
        out_shape=jax.ShapeDtypeStruct((batch, N_EXPERTS, seq), F32),
        compiler_params=_params(("parallel", "parallel")),
        name="router",
    )(x2, g, wt)


def _cumsum_lanes(x):
    n = x.shape[1]
    lane = lax.broadcasted_iota(jnp.int32, x.shape, 1)
    sh = 1
    while sh < n:
        x = x + jnp.where(lane >= sh, pltpu.roll(x, sh, axis=1), 0.0)
        sh *= 2
    return x


def _select_kernel(aff_ref, o_ref, *, cap):
    aff = aff_ref[0]
    seq = aff.shape[1]
    bits = pltpu.bitcast(aff, jnp.int32)
    thr = jnp.zeros((N_EXPERTS, 1), jnp.int32)
    for bit in range(30, -1, -1):
        cand = thr | (1 << bit)
        cnt = jnp.sum((bits >= cand).astype(jnp.int32), axis=1, keepdims=True)
        thr = jnp.where(cnt >= cap, cand, thr)
    gt = bits > thr
    eq = bits == thr
    need = cap - jnp.sum(gt.astype(F32), axis=1, keepdims=True)
    sel = gt | (eq & (_cumsum_lanes(eq.astype(F32)) <= need))
    slot = jnp.where(sel, _cumsum_lanes(sel.astype(F32)) - 1.0, -1.0)

    hi = aff.astype(BF16).astype(F32)
    r1 = aff - hi
    mid = r1.astype(BF16).astype(F32)
    lo = (r1 - mid).astype(BF16).astype(F32)
    tok = lax.broadcasted_iota(jnp.int32, (1, seq), 1)
    t_hi = (tok >> 6).astype(F32)
    t_lo = (tok & 63).astype(F32)
    pad = jnp.zeros((SELECT_COLS - 5, seq), F32)
    cblk = 128
    for e in range(N_EXPERTS):
        rhs = jnp.concatenate([hi[e:e + 1], mid[e:e + 1], lo[e:e + 1], t_hi, t_lo, pad], axis=0).astype(BF16)
        slot_e = slot[e:e + 1, :]
        for cc in range(cap // cblk):
            ci = (lax.broadcasted_iota(jnp.int32, (cblk, 1), 0) + cc * cblk).astype(F32)
            onehot = jnp.where(slot_e == ci, 1.0, 0.0).astype(BF16)
            o_ref[0, e, cc * cblk:(cc + 1) * cblk, :] = _dot_nt(onehot, rhs)


def _select(aff_t, cap):
    batch, _, seq = aff_t.shape
    return pl.pallas_call(
        functools.partial(_select_kernel, cap=cap),
        grid=(batch,),
        in_specs=[pl.BlockSpec((1, N_EXPERTS, seq), lambda b: (b, 0, 0))],
        out_specs=pl.BlockSpec((1, N_EXPERTS, cap, SELECT_COLS), lambda b: (b, 0, 0, 0)),
        out_shape=jax.ShapeDtypeStruct((batch, N_EXPERTS, cap, SELECT_COLS), F32),
        compiler_params=_params(("parallel",)),
        name="moe_select",
    )(aff_t)


def _moe_kernel(idx_ref, x_hbm, val_ref, g_ref, w1_ref, w3_ref, w2_ref, o_hbm,
                xs_ref, acc_ref, tile_ref, xg_ref, y_ref, sem, *, cap):
    b = pl.program_id(0)
    e = pl.program_id(1)
    f = pl.program_id(2)
    last_f = pl.num_programs(2) - 1
    nchunk = D_MODEL // LANES

    @pl.when((e == 0) & (f == 0))
    def _load_batch_row():
        c0 = pltpu.make_async_copy(x_hbm.at[b], xs_ref, sem.at[0])
        c1 = pltpu.make_async_copy(x_hbm.at[b], acc_ref, sem.at[1])
        c0.start()
        c1.start()
        c0.wait()
        c1.wait()

    @pl.when(f == 0)
    def _gather():
        def body(i, carry):
            for u in range(GATHER_UNROLL):
                c = i * GATHER_UNROLL + u
                row = pl.multiple_of(idx_ref[0, 0, c] * SUBLANES, SUBLANES)
                tile_ref[pl.ds(c, nchunk, stride=TILE_PITCH), :] = xs_ref[pl.ds(row, SUBLANES), :]
            return carry

        lax.fori_loop(0, cap // GATHER_UNROLL, body, 0)
        xg = jnp.concatenate([tile_ref[pl.ds(j * TILE_PITCH, cap), :] for j in range(nchunk)], axis=1)
        xg_ref[...] = _rms(xg, g_ref[...]).astype(BF16)

    xg = xg_ref[...]
    h1 = _dot(xg, w1_ref[0])
    h3 = _dot(xg, w3_ref[0])
    act = (h1 * _sigmoid(h1)) * h3
    part = _dot(act.astype(BF16), w2_ref[0])

    @pl.when(f == 0)
    def _first():
        y_ref[...] = part

    @pl.when(f != 0)
    def _rest():
        y_ref[...] += part

    @pl.when(f == last_f)
    def _scatter():
        y = y_ref[...] * val_ref[0]
        for j in range(nchunk):
            tile_ref[pl.ds(j * TILE_PITCH, cap), :] = y[:, j * LANES:(j + 1) * LANES]

        def body(i, carry):
            rows, new = [], []
            for u in range(GATHER_UNROLL):
                c = i * GATHER_UNROLL + u
                row = pl.multiple_of(idx_ref[0, 0, c] * SUBLANES, SUBLANES)
                rows.append(row)
                new.append(acc_ref[pl.ds(row, SUBLANES), :] + tile_ref[pl.ds(c, nchunk, stride=TILE_PITCH), :])
            for row, val in zip(rows, new):
                acc_ref[pl.ds(row, SUBLANES), :] = val
            return carry

        lax.fori_loop(0, cap // GATHER_UNROLL, body, 0)

    @pl.when((e == pl.num_programs(1) - 1) & (f == last_f))
    def _store_batch_row():
        c = pltpu.make_async_copy(acc_ref, o_hbm.at[b], sem.at[0])
        c.start()
        c.wait()


def _moe(x_slab, idx, val, g, w1, w3, w2, cap):
    batch, srows, _ = x_slab.shape
    nf = EXPERT_FF // MOE_TF
    return pl.pallas_call(
        functools.partial(_moe_kernel, cap=cap),
        grid=(batch, N_EXPERTS, nf),
        in_specs=[
            pl.BlockSpec((1, 1, cap), lambda b, e, f: (b * N_EXPERTS + e, 0, 0), memory_space=pltpu.SMEM),
            pl.BlockSpec(memory_space=pl.ANY),
            pl.BlockSpec((1, cap, 1), lambda b, e, f: (b * N_EXPERTS + e, 0, 0)),
            pl.BlockSpec((1, D_MODEL), lambda b, e, f: (0, 0)),
            pl.BlockSpec((1, D_MODEL, MOE_TF), lambda b, e, f: (e, 0, f)),
            pl.BlockSpec((1, D_MODEL, MOE_TF), lambda b, e, f: (e, 0, f)),
            pl.BlockSpec((1, MOE_TF, D_MODEL), lambda b, e, f: (e, f, 0)),
        ],
        out_specs=pl.BlockSpec(memory_space=pl.ANY),
        out_shape=jax.ShapeDtypeStruct(x_slab.shape, F32),
        scratch_shapes=[
            pltpu.VMEM((srows, LANES), F32),
            pltpu.VMEM((srows, LANES), F32),
            pltpu.VMEM((D_MODEL // LANES * TILE_PITCH, LANES), F32),
            pltpu.VMEM((cap, D_MODEL), BF16),
            pltpu.VMEM((cap, D_MODEL), F32),
            pltpu.SemaphoreType.DMA((2,)),
        ],
        compiler_params=_params(("arbitrary", "arbitrary", "arbitrary")),
        name="moe_ffn",
    )(idx, x_slab, val, g, w1, w3, w2)


def _ple_kernel(x_ref, p_ref, g_ref, wg_ref, wp_ref, gf_ref, o_ref, *, final):
    x = x_ref[...]
    gate = _sigmoid(_dot(_rms(x, g_ref[...]).astype(BF16), wg_ref[...]))
    y = x + gate * _dot(p_ref[...].astype(BF16), wp_ref[...])
    if final:
        y = _rms(y, gf_ref[...])
    o_ref[...] = y


def _ple(x2, p2, g, wg, wp, gf, final):
    t = x2.shape[0]
    full = lambda shape: pl.BlockSpec(shape, lambda i: (0, 0))
    return pl.pallas_call(
        functools.partial(_ple_kernel, final=final),
        grid=(t // TM,),
        in_specs=[
            pl.BlockSpec((TM, D_MODEL), lambda i: (i, 0)),
            pl.BlockSpec((TM, PLE_DIM), lambda i: (i, 0)),
            full((1, D_MODEL)), full((D_MODEL, D_MODEL)), full((PLE_DIM, D_MODEL)), full((1, D_MODEL)),
        ],
        out_specs=pl.BlockSpec((TM, D_MODEL), lambda i: (i, 0)),
        out_shape=jax.ShapeDtypeStruct((t, D_MODEL), F32),
        compiler_params=_params(("parallel",)),
        name="ple",
    )(x2, p2, g, wg, wp, gf)


def _rot_cols(w):
    q = MLA_ROPE // 4
    return jnp.concatenate([-w[:, q:2 * q], w[:, :q], -w[:, 3 * q:], w[:, 2 * q:3 * q]], axis=1)


def _prep_w_in(w):
    cuts = np.cumsum([NA_WIDTH, NA_WIDTH, NA_WIDTH, MLA_Q_RANK, MLA_KV_RANK, MLA_ROPE, D_MODEL])
    na_q, na_k, na_v, q_lat, kv_lat, k_rope, gate_a, gate_b = jnp.split(w, [int(c) for c in cuts], axis=1)
    z = lambda n: jnp.zeros((D_MODEL, n), w.dtype)
    out = jnp.concatenate([
        gate_a, gate_b, na_q, na_k, na_v,
        q_lat, z(Q_LAT_PAD - MLA_Q_RANK),
        kv_lat,
        z(MLA_NOPE), k_rope, z(HEAD_SLOT - MLA_NOPE - MLA_ROPE),
        z(MLA_NOPE), _rot_cols(k_rope), z(HEAD_SLOT - MLA_NOPE - MLA_ROPE),
    ], axis=1)
    assert out.shape[1] == N_PROJ
    return out.astype(BF16)


def _prep_mla_weights(wq_up, wkv_up):
    dqk = MLA_NOPE + MLA_ROPE
    wq = wq_up.reshape(MLA_Q_RANK, MLA_HEADS, dqk)
    nope, pe = wq[..., :MLA_NOPE], wq[..., MLA_NOPE:]
    pe_rot = _rot_cols(pe.reshape(MLA_Q_RANK * MLA_HEADS, MLA_ROPE)).reshape(MLA_Q_RANK, MLA_HEADS, MLA_ROPE)
    zq = lambda n: jnp.zeros((MLA_Q_RANK, MLA_HEADS, n), wq_up.dtype)
    tail = HEAD_SLOT - dqk
    wq_pad = jnp.concatenate([nope, pe, zq(tail)], axis=2).reshape(MLA_Q_RANK, MLA_HEADS * HEAD_SLOT)
    wq_rot = jnp.concatenate([zq(MLA_NOPE), pe_rot, zq(tail)], axis=2).reshape(MLA_Q_RANK, MLA_HEADS * HEAD_SLOT)
    rowpad = ((0, Q_LAT_PAD - MLA_Q_RANK), (0, 0))
    wq_pad = jnp.pad(wq_pad, rowpad).astype(BF16)
    wq_rot = jnp.pad(wq_rot, rowpad).astype(BF16)

    wkv = wkv_up.reshape(MLA_KV_RANK, MLA_HEADS, MLA_NOPE + MLA_V)
    k_nope, v = wkv[..., :MLA_NOPE], wkv[..., MLA_NOPE:]
    zk = jnp.zeros((MLA_KV_RANK, MLA_HEADS, HEAD_SLOT - MLA_NOPE), wkv_up.dtype)
    wk = jnp.concatenate([k_nope, zk], axis=2).reshape(MLA_KV_RANK, MLA_HEADS * HEAD_SLOT).astype(BF16)
    zv = jnp.zeros_like(v)
    even = (jnp.arange(MLA_HEADS) % 2 == 0)[None, :, None]
    wv = jnp.concatenate([jnp.where(even, v, zv), jnp.where(even, zv, v)], axis=2)
    wv = wv.reshape(MLA_KV_RANK, MLA_HEADS * HEAD_SLOT).astype(BF16)
    lane = np.arange(MLA_HEADS * HEAD_SLOT)
    head_even = (lane // HEAD_SLOT) % 2 == 0
    upper = (lane % HEAD_SLOT) >= MLA_V
    ones = jnp.asarray(np.where(head_even == upper, 1.0, 0.0).astype(np.float32))[None, :]
    return wq_pad, wq_rot, wk, wv, ones


def _rope_tables(seq):
    t = np.arange(seq)
    half = MLA_ROPE // 2
    freqs = 1.0 / (ROPE_BASE ** (jnp.arange(0, half, 2, dtype=F32) / half))

    def tab(pos):
        ang = jnp.asarray(pos, F32)[:, None] * freqs[None, :]
        return jnp.concatenate([jnp.cos(ang)] * 2, axis=1), jnp.concatenate([jnp.sin(ang)] * 2, axis=1)

    cr, sr = tab(t // GRID_W)
    cc, sc = tab(t % GRID_W)
    tail = HEAD_SLOT - MLA_NOPE - MLA_ROPE
    cos_t = jnp.concatenate([jnp.ones((seq, MLA_NOPE), F32), cr, cc, jnp.zeros((seq, tail), F32)], axis=1)
    sin_t = jnp.concatenate([jnp.zeros((seq, MLA_NOPE), F32), sr, sc, jnp.zeros((seq, tail), F32)], axis=1)
    return cos_t, sin_t


def kernel(x, p, norm_mix, w_in, na_rpb, mla_q_norm, mla_wq_up, mla_kv_norm, mla_wkv_up, w_na_o, w_mla_o, w_out,
           norm_moe, w_router, moe_w1, moe_w3, moe_w2, norm_ple, ple_gate_w, ple_w, norm_final):
    batch, seq, d = x.shape
    depth = w_in.shape[0]
    assert d == D_MODEL and seq % (GRID_W * NA_ROWS) == 0 and seq % TM == 0
    t = batch * seq
    cap = EC_CAPACITY * seq // N_EXPERTS
    assert TILE_PITCH == cap + SUBLANES and seq <= 64 * 64
    rows = seq // GRID_W
    cos_t, sin_t = _rope_tables(seq)
    row = lambda v: v.reshape(1, -1).astype(F32)

    x2 = x.reshape(t, d)
    for i in range(depth):
        proj = _in_proj(x2, row(norm_mix[i]), _prep_w_in(w_in[i]))
        ya = _na_attention(proj, _na_bias_tables(na_rpb[i], rows), batch, seq)
        wq, wqr, wk, wv, ones = _prep_mla_weights(mla_wq_up[i], mla_wkv_up[i])
        qn = jnp.pad(row(mla_q_norm[i]), ((0, 0), (0, Q_LAT_PAD - MLA_Q_RANK)))
        qp, kp, vp = _mla_prep(proj, qn, row(mla_kv_norm[i]), wq, wqr, wk, wv, ones, cos_t, sin_t, seq)
        yb = _flash(qp, kp, vp, batch, seq)
        x2 = _merge(x2, ya, yb, proj, w_na_o[i].astype(BF16), w_mla_o[i].astype(BF16), w_out[i].astype(BF16))

        aff_t = _router(x2, row(norm_moe[i]), w_router[i].T.astype(F32), batch, seq)
        picked = _select(aff_t, cap)
        val = (picked[..., 0] + picked[..., 1]) + picked[..., 2]
        idx = (picked[..., 3] * 64.0 + picked[..., 4]).astype(jnp.int32)
        x_slab = x2.reshape(batch, seq * (d // LANES), LANES)
        x_slab = _moe(x_slab, idx.reshape(batch * N_EXPERTS, 1, cap), val.reshape(batch * N_EXPERTS, cap, 1),
                      row(norm_moe[i]), moe_w1[i].astype(BF16), moe_w3[i].astype(BF16), moe_w2[i].astype(BF16), cap)
        x2 = x_slab.reshape(t, d)

        x2 = _ple(x2, p[i].reshape(t, PLE_DIM), row(norm_ple[i]), ple_gate_w[i].astype(BF16),
                  ple_w[i].astype(BF16), row(norm_final), final=(i == depth - 1))
    return x2.reshape(batch, seq, d)
```

```python
import functools
import math

import jax
import jax.numpy as jnp
import numpy as np
from jax import lax
from jax.experimental import pallas as pl
from jax.experimental.pallas import tpu as pltpu

F32 = jnp.float32
BF16 = jnp.bfloat16

D_MODEL = 1024
GRID_W = 64
NA_HEADS = 8
NA_HEAD_DIM = 64
NA_WIN_ROWS = 8
NA_WIN_COLS = 16
NA_WIDTH = NA_HEADS * NA_HEAD_DIM
MLA_HEADS = 8
MLA_Q_RANK = 384
MLA_KV_RANK = 256
MLA_NOPE = 64
MLA_ROPE = 32
MLA_V = 64
MLA_WIDTH = MLA_HEADS * MLA_V
ROPE_BASE = 10000.0
N_EXPERTS = 16
EC_CAPACITY = 2
EXPERT_FF = 1024
PLE_DIM = 256
RMS_EPS = 1e-6

LANES = 128
SUBLANES = 8
HEAD_SLOT = 128
VMEM_LIMIT = 56 * 1024 * 1024

C_GATE_A = 0
C_GATE_B = 1024
C_NA_Q = 2048
C_NA_K = 2560
C_NA_V = 3072
C_Q_LAT = 3584
Q_LAT_PAD = 512
C_KV_LAT = 4096
C_K_ROPE = 4352
N_PROJ = 4608

TM = 512
NA_ROWS = 4
NA_KROWS = 12
FLASH_TQ = 256
FLASH_TK = 512
FLASH_SUBTILES = 2
MOE_TF = 512
TILE_PITCH = 520
GATHER_UNROLL = 8
NEG_BIG = -1e30
SELECT_COLS = 16

MLA_QSCALE = float((MLA_NOPE + MLA_ROPE) ** -0.5 * math.log2(math.e))


def _dot(a, b):
    return jnp.dot(a, b, preferred_element_type=F32)


def _dot_nt(a, b):
    return lax.dot_general(a, b, (((1,), (1,)), ((), ())), preferred_element_type=F32)


def _rms(x, g, n=None):
    n = x.shape[-1] if n is None else n
    ms = jnp.sum(x * x, axis=-1, keepdims=True) * (1.0 / n)
    return (x * lax.rsqrt(ms + RMS_EPS)) * g


def _sigmoid(x):
    return 1.0 / (1.0 + jnp.exp(-x))


def _params(sem, vmem=VMEM_LIMIT):
    return pltpu.CompilerParams(dimension_semantics=sem, vmem_limit_bytes=vmem)


def _in_proj_kernel(x_ref, g_ref, w_ref, o_ref):
    h = _rms(x_ref[...], g_ref[...]).astype(BF16)
    for n in range(N_PROJ // 256):
        sl = slice(n * 256, (n + 1) * 256)
        o_ref[:, sl] = _dot(h, w_ref[:, sl]).astype(BF16)


def _in_proj(x2, g, w):
    t = x2.shape[0]
    return pl.pallas_call(
        _in_proj_kernel,
        grid=(t // TM,),
        in_specs=[
            pl.BlockSpec((TM, D_MODEL), lambda i: (i, 0)),
            pl.BlockSpec((1, D_MODEL), lambda i: (0, 0)),
            pl.BlockSpec((D_MODEL, N_PROJ), lambda i: (0, 0)),
        ],
        out_specs=pl.BlockSpec((TM, N_PROJ), lambda i: (i, 0)),
        out_shape=jax.ShapeDtypeStruct((t, N_PROJ), BF16),
        compiler_params=_params(("parallel",)),
        name="in_proj",
    )(x2, g, w)


def _na_kernel(q_ref, k_ref, v_ref, b_ref, o_ref):
    m = pl.program_id(2)
    rows = k_ref.shape[0] // GRID_W
    kr0 = jnp.clip(NA_ROWS * m - NA_ROWS, 0, rows - NA_KROWS)
    start = pl.multiple_of(kr0 * GRID_W, NA_ROWS * GRID_W)
    nk = NA_KROWS * GRID_W
    k = k_ref[pl.ds(start, nk), :]
    v = v_ref[pl.ds(start, nk), :]
    q = q_ref[...]
    lane = lax.broadcasted_iota(jnp.int32, q.shape, 1)
    scale = NA_HEAD_DIM ** -0.5
    outs = []
    for h in range(2):
        in_head = (lane >= h * NA_HEAD_DIM) & (lane < (h + 1) * NA_HEAD_DIM)
        qh = jnp.where(in_head, q, jnp.zeros_like(q)) * scale
        s = _dot_nt(qh, k) + b_ref[0, h]
        mx = jnp.max(s, axis=1, keepdims=True)
        p = jnp.exp(s - mx)
        l = jnp.sum(p, axis=1, keepdims=True)
        outs.append(_dot(p.astype(BF16), v) / l)
    o_ref[...] = jnp.where(lane < NA_HEAD_DIM, outs[0], outs[1]).astype(BF16)


def _na_attention(proj, bias, batch, seq):
    rows = seq // GRID_W
    nblk = rows // NA_ROWS
    tq = NA_ROWS * GRID_W

    def variant(m):
        return jnp.where(m == 0, 0, jnp.where(m == nblk - 1, 2, 1))

    return pl.pallas_call(
        _na_kernel,
        grid=(batch, NA_HEADS // 2, nblk),
        in_specs=[
            pl.BlockSpec((tq, LANES), lambda b, hp, m: (b * nblk + m, C_NA_Q // LANES + hp)),
            pl.BlockSpec((seq, LANES), lambda b, hp, m: (b, C_NA_K // LANES + hp)),
            pl.BlockSpec((seq, LANES), lambda b, hp, m: (b, C_NA_V // LANES + hp)),
            pl.BlockSpec((1, 2, tq, NA_KROWS * GRID_W), lambda b, hp, m: (variant(m), hp, 0, 0)),
        ],
        out_specs=pl.BlockSpec((tq, LANES), lambda b, hp, m: (b * nblk + m, hp)),
        out_shape=jax.ShapeDtypeStruct((batch * seq, NA_WIDTH), BF16),
        compiler_params=_params(("parallel", "parallel", "arbitrary")),
        name="na_attention",
    )(proj, proj, proj, bias)


def _na_bias_tables(rpb, rows):
    nblk = rows // NA_ROWS
    qr = np.arange(NA_ROWS)
    kk = np.arange(NA_KROWS)
    row_sel = np.zeros((3, NA_ROWS, NA_KROWS, 2 * NA_WIN_ROWS - 1), np.float32)
    row_ok = np.zeros((3, NA_ROWS, NA_KROWS), bool)
    for vi, m in enumerate((0, 1, nblk - 1)):
        r = NA_ROWS * m + qr
        r0 = np.clip(r - NA_WIN_ROWS // 2, 0, rows - NA_WIN_ROWS)
        kr0 = np.clip(NA_ROWS * m - NA_ROWS, 0, rows - NA_KROWS)
        key_row = kr0 + kk
        ok = (key_row[None, :] >= r0[:, None]) & (key_row[None, :] < r0[:, None] + NA_WIN_ROWS)
        a = np.clip(key_row[None, :] - r[:, None] + NA_WIN_ROWS - 1, 0, 2 * NA_WIN_ROWS - 2)
        row_ok[vi] = ok
        row_sel[vi, qr[:, None], kk[None, :], a] = 1.0
    c = np.arange(GRID_W)
    cs = np.clip(c - NA_WIN_COLS // 2, 0, GRID_W - NA_WIN_COLS)
    col_ok = (c[None, :] >= cs[:, None]) & (c[None, :] < cs[:, None] + NA_WIN_COLS)
    bc = np.clip(c[None, :] - c[:, None] + NA_WIN_COLS - 1, 0, 2 * NA_WIN_COLS - 2)
    col_sel = np.zeros((GRID_W, GRID_W, 2 * NA_WIN_COLS - 1), np.float32)
    col_sel[c[:, None], c[None, :], bc] = 1.0
    bias = jnp.einsum("vqka,hab,cjb->vhqckj", jnp.asarray(row_sel), rpb.astype(F32), jnp.asarray(col_sel),
                      precision=lax.Precision.HIGHEST)
    ok = row_ok[:, None, :, None, :, None] & col_ok[None, None, None, :, None, :]
    bias = jnp.where(jnp.asarray(ok), bias, NEG_BIG)
    return bias.reshape(3, NA_HEADS, NA_ROWS * GRID_W, NA_KROWS * GRID_W)


def _mla_prep_kernel(ql_ref, kvl_ref, kr_ref, qn_ref, kvn_ref, wq_ref, wqr_ref, wk_ref, wv_ref,
                     ones_ref, cos_ref, sin_ref, q_out, k_out, v_out):
    cos = cos_ref[...]
    sin = sin_ref[...]
    cos8 = jnp.concatenate([cos] * MLA_HEADS, axis=1)
    sin8 = jnp.concatenate([sin] * MLA_HEADS, axis=1)
    hq = _rms(ql_ref[...].astype(F32), qn_ref[...], MLA_Q_RANK).astype(BF16)
    q = _dot(hq, wq_ref[...]) * cos8 + _dot(hq, wqr_ref[...]) * sin8
    q_out[...] = (q * MLA_QSCALE).astype(BF16)
    hkv = _rms(kvl_ref[...].astype(F32), kvn_ref[...]).astype(BF16)
    kr = kr_ref[...].astype(F32)
    kpe = kr[:, :HEAD_SLOT] * cos + kr[:, HEAD_SLOT:] * sin
    k = _dot(hkv, wk_ref[...]) + jnp.concatenate([kpe] * MLA_HEADS, axis=1)
    k_out[0] = k.T.astype(BF16)
    v_out[...] = (_dot(hkv, wv_ref[...]) + ones_ref[...]).astype(BF16)


def _mla_prep(proj, qn, kvn, wq, wqr, wk, wv, ones, cos_t, sin_t, seq):
    t = proj.shape[0]
    width = MLA_HEADS * HEAD_SLOT
    sblk = seq // TM
    full = lambda shape: pl.BlockSpec(shape, lambda i: (0, 0))
    out = jax.ShapeDtypeStruct((t, width), BF16)
    return pl.pallas_call(
        _mla_prep_kernel,
        grid=(t // TM,),
        in_specs=[
            pl.BlockSpec((TM, Q_LAT_PAD), lambda i: (i, C_Q_LAT // Q_LAT_PAD)),
            pl.BlockSpec((TM, MLA_KV_RANK), lambda i: (i, C_KV_LAT // MLA_KV_RANK)),
            pl.BlockSpec((TM, 2 * HEAD_SLOT), lambda i: (i, C_K_ROPE // (2 * HEAD_SLOT))),
            full((1, Q_LAT_PAD)), full((1, MLA_KV_RANK)),
            full((Q_LAT_PAD, width)), full((Q_LAT_PAD, width)),
            full((MLA_KV_RANK, width)), full((MLA_KV_RANK, width)),
            full((1, width)),
            pl.BlockSpec((TM, HEAD_SLOT), lambda i: (i % sblk, 0)),
            pl.BlockSpec((TM, HEAD_SLOT), lambda i: (i % sblk, 0)),
        ],
        out_specs=[pl.BlockSpec((TM, width), lambda i: (i, 0)),
                   pl.BlockSpec((1, width, TM), lambda i: (i // sblk, 0, i % sblk)),
                   pl.BlockSpec((TM, width), lambda i: (i, 0))],
        out_shape=[out, jax.ShapeDtypeStruct((t // seq, width, seq), BF16), out],
        compiler_params=_params(("parallel",)),
        name="mla_prep",
    )(proj, proj, proj, qn, kvn, wq, wqr, wk, wv, ones, cos_t, sin_t)


def _flash_kernel(q_ref, kt_ref, v_ref, o_ref):
    seq = v_ref.shape[0]
    nsub = q_ref.shape[0] // FLASH_TQ
    hslices = [slice(h * HEAD_SLOT, (h + 1) * HEAD_SLOT) for h in range(2)]
    streams = [(qi, h) for qi in range(nsub) for h in range(2)]
    qs = {(qi, h): q_ref[qi * FLASH_TQ:(qi + 1) * FLASH_TQ, hslices[h]] for qi, h in streams}
    ms = {st: jnp.full((FLASH_TQ, 1), NEG_BIG, F32) for st in streams}
    accs = {st: jnp.zeros((FLASH_TQ, HEAD_SLOT), F32) for st in streams}
    items = [(j, st) for j in range(seq // FLASH_TK) for st in streams]

    def scores(item):
        j, st = item
        return _dot(qs[st], kt_ref[0, hslices[st[1]], j * FLASH_TK:(j + 1) * FLASH_TK])

    s_next = scores(items[0])
    for n, (j, st) in enumerate(items):
        s = s_next
        if n + 1 < len(items):
            s_next = scores(items[n + 1])
        v = v_ref[j * FLASH_TK:(j + 1) * FLASH_TK, hslices[st[1]]]
        m_new = jnp.maximum(ms[st], jnp.max(s, axis=1, keepdims=True))
        p = jnp.exp2((s - m_new).astype(BF16))
        alpha = jnp.exp2(ms[st] - m_new)
        accs[st] = alpha * accs[st] + _dot(p, v)
        ms[st] = m_new
    lane = lax.broadcasted_iota(jnp.int32, (FLASH_TQ, HEAD_SLOT), 1)
    for qi in range(nsub):
        o0, o1 = [accs[(qi, h)] / pltpu.roll(accs[(qi, h)], MLA_V, axis=1) for h in range(2)]
        o_ref[qi * FLASH_TQ:(qi + 1) * FLASH_TQ, :] = jnp.where(lane < MLA_V, o0, o1).astype(BF16)


def _flash(qp, kp, vp, batch, seq):
    tq = FLASH_TQ * FLASH_SUBTILES
    nq = seq // tq
    return pl.pallas_call(
        _flash_kernel,
        grid=(batch, MLA_HEADS // 2, nq),
        in_specs=[
            pl.BlockSpec((tq, 2 * HEAD_SLOT), lambda b, hp, i: (b * nq + i, hp)),
            pl.BlockSpec((1, 2 * HEAD_SLOT, seq), lambda b, hp, i: (b, hp, 0)),
            pl.BlockSpec((seq, 2 * HEAD_SLOT), lambda b, hp, i: (b, hp)),
        ],
        out_specs=pl.BlockSpec((tq, LANES), lambda b, hp, i: (b * nq + i, hp)),
        out_shape=jax.ShapeDtypeStruct((batch * seq, MLA_WIDTH), BF16),
        compiler_params=_params(("parallel", "parallel", "arbitrary")),
        name="mla_flash",
    )(qp, kp, vp)


def _to_slab(o_ref, x):
    for j in range(D_MODEL // LANES):
        o_ref[pl.ds(j, x.shape[0], stride=D_MODEL // LANES), :] = x[:, j * LANES:(j + 1) * LANES]


def _from_slab(x_ref, rows):
    n = D_MODEL // LANES
    return jnp.concatenate([x_ref[pl.ds(j, rows, stride=n), :] for j in range(n)], axis=1)


def _merge_kernel(x_ref, ya_ref, yb_ref, ga_ref, gb_ref, wa_ref, wb_ref, wo_ref, gm_ref, wr_ref, o_ref, aff_ref):
    ya = _dot(ya_ref[...], wa_ref[...])
    yb = _dot(yb_ref[...], wb_ref[...])
    merged = _sigmoid(ga_ref[...].astype(F32)) * ya + _sigmoid(gb_ref[...].astype(F32)) * yb
    x = x_ref[...] + _dot(merged.astype(BF16), wo_ref[...])
    _to_slab(o_ref, x)
    h = _rms(x, gm_ref[...])
    h_hi = h.astype(BF16)
    h_lo = (h - h_hi.astype(F32)).astype(BF16)
    r = _dot(h_hi, wr_ref[...])
    logits = (r[:, :LANES] + r[:, LANES:]) + _dot(h_lo, wr_ref[:, :LANES])
    lane = lax.broadcasted_iota(jnp.int32, logits.shape, 1)
    logits = jnp.where(lane < N_EXPERTS, logits, NEG_BIG)
    e = jnp.exp(logits - jnp.max(logits, axis=1, keepdims=True))
    aff = e / jnp.sum(e, axis=1, keepdims=True)
    aff_ref[0] = aff.T[:N_EXPERTS, :]


def _merge(x2, ya, yb, proj, wa, wb, wo, gm, wr, seq):
    t = x2.shape[0]
    sblk = seq // TM
    nslab = D_MODEL // LANES
    full = lambda shape: pl.BlockSpec(shape, lambda i: (0, 0))
    return pl.pallas_call(
        _merge_kernel,
        grid=(t // TM,),
        in_specs=[
            pl.BlockSpec((TM, D_MODEL), lambda i: (i, 0)),
            pl.BlockSpec((TM, NA_WIDTH), lambda i: (i, 0)),
            pl.BlockSpec((TM, MLA_WIDTH), lambda i: (i, 0)),
            pl.BlockSpec((TM, D_MODEL), lambda i: (i, C_GATE_A // D_MODEL)),
            pl.BlockSpec((TM, D_MODEL), lambda i: (i, C_GATE_B // D_MODEL)),
            full((NA_WIDTH, D_MODEL)), full((MLA_WIDTH, D_MODEL)), full((D_MODEL, D_MODEL)),
            full((1, D_MODEL)), full((D_MODEL, 2 * LANES)),
        ],
        out_specs=[pl.BlockSpec((TM * nslab, LANES), lambda i: (i, 0)),
                   pl.BlockSpec((1, N_EXPERTS, TM), lambda i: (i // sblk, 0, i % sblk))],
        out_shape=[jax.ShapeDtypeStruct((t * nslab, LANES), F32),
                   jax.ShapeDtypeStruct((t // seq, N_EXPERTS, seq), F32)],
        compiler_params=_params(("parallel",)),
        name="merge",
    )(x2, ya, yb, proj, proj, wa, wb, wo, gm, wr)


def _prep_router(w):
    hi = w.astype(BF16)
    lo = (w - hi.astype(F32)).astype(BF16)
    z = jnp.zeros((D_MODEL, LANES - N_EXPERTS), BF16)
    return jnp.concatenate([hi, z, lo, z], axis=1)


def _cumsum_lanes(x):
    n = x.shape[1]
    lane = lax.broadcasted_iota(jnp.int32, x.shape, 1)
    sh = 1
    while sh < n:
        x = x + jnp.where(lane >= sh, pltpu.roll(x, sh, axis=1), 0.0)
        sh *= 2
    return x


def _select_kernel(aff_ref, o_ref, *, cap):
    aff = aff_ref[0]
    seq = aff.shape[1]
    bits = pltpu.bitcast(aff, jnp.int32)
    thr = jnp.zeros((N_EXPERTS, 1), jnp.int32)
    for bit in range(30, -1, -1):
        cand = thr | (1 << bit)
        cnt = jnp.sum((bits >= cand).astype(jnp.int32), axis=1, keepdims=True)
        thr = jnp.where(cnt >= cap, cand, thr)
    gt = bits > thr
    eq = bits == thr
    need = cap - jnp.sum(gt.astype(F32), axis=1, keepdims=True)
    sel = gt | (eq & (_cumsum_lanes(eq.astype(F32)) <= need))
    slot = jnp.where(sel, _cumsum_lanes(sel.astype(F32)) - 1.0, -1.0)

    hi = aff.astype(BF16).astype(F32)
    r1 = aff - hi
    mid = r1.astype(BF16).astype(F32)
    lo = (r1 - mid).astype(BF16).astype(F32)
    tok = lax.broadcasted_iota(jnp.int32, (1, seq), 1)
    t_hi = (tok >> 6).astype(F32)
    t_lo = (tok & 63).astype(F32)
    pad = jnp.zeros((SELECT_COLS - 5, seq), F32)
    cblk = 128
    for e in range(N_EXPERTS):
        rhs = jnp.concatenate([hi[e:e + 1], mid[e:e + 1], lo[e:e + 1], t_hi, t_lo, pad], axis=0).astype(BF16)
        slot_e = slot[e:e + 1, :]
        for cc in range(cap // cblk):
            ci = (lax.broadcasted_iota(jnp.int32, (cblk, 1), 0) + cc * cblk).astype(F32)
            onehot = jnp.where(slot_e == ci, 1.0, 0.0).astype(BF16)
            o_ref[0, e, cc * cblk:(cc + 1) * cblk, :] = _dot_nt(onehot, rhs)


def _select(aff_t, cap):
    batch, _, seq = aff_t.shape
    return pl.pallas_call(
        functools.partial(_select_kernel, cap=cap),
        grid=(batch,),
        in_specs=[pl.BlockSpec((1, N_EXPERTS, seq), lambda b: (b, 0, 0))],
        out_specs=pl.BlockSpec((1, N_EXPERTS, cap, SELECT_COLS), lambda b: (b, 0, 0, 0)),
        out_shape=jax.ShapeDtypeStruct((batch, N_EXPERTS, cap, SELECT_COLS), F32),
        compiler_params=_params(("parallel",)),
        name="moe_select",
    )(aff_t)


def _moe_kernel(idx_ref, x_hbm, val_ref, g_ref, w1_ref, w3_ref, w2_ref, o_hbm,
                xs_ref, acc_ref, tile_ref, xg_ref, y_ref, sem, *, cap):
    b = pl.program_id(0)
    e = pl.program_id(1)
    f = pl.program_id(2)
    last_f = pl.num_programs(2) - 1
    nchunk = D_MODEL // LANES

    @pl.when((e == 0) & (f == 0))
    def _load_batch_row():
        c0 = pltpu.make_async_copy(x_hbm.at[b], xs_ref, sem.at[0])
        c1 = pltpu.make_async_copy(x_hbm.at[b], acc_ref, sem.at[1])
        c0.start()
        c1.start()
        c0.wait()
        c1.wait()

    @pl.when(f == 0)
    def _gather():
        def body(i, carry):
            for u in range(GATHER_UNROLL):
                c = i * GATHER_UNROLL + u
                row = pl.multiple_of(idx_ref[0, 0, c] * SUBLANES, SUBLANES)
                tile_ref[pl.ds(c, nchunk, stride=TILE_PITCH), :] = xs_ref[pl.ds(row, SUBLANES), :]
            return carry

        lax.fori_loop(0, cap // GATHER_UNROLL, body, 0)
        xg = jnp.concatenate([tile_ref[pl.ds(j * TILE_PITCH, cap), :] for j in range(nchunk)], axis=1)
        xg_ref[...] = _rms(xg, g_ref[...]).astype(BF16)

    xg = xg_ref[...]
    h1 = _dot(xg, w1_ref[0])
    h3 = _dot(xg, w3_ref[0])
    act = (h1 * _sigmoid(h1)) * h3
    part = _dot(act.astype(BF16), w2_ref[0])

    @pl.when(f == 0)
    def _first():
        y_ref[...] = part

    @pl.when(f != 0)
    def _rest():
        y_ref[...] += part

    @pl.when(f == last_f)
    def _scatter():
        y = y_ref[...] * val_ref[0]
        for j in range(nchunk):
            tile_ref[pl.ds(j * TILE_PITCH, cap), :] = y[:, j * LANES:(j + 1) * LANES]

        def body(i, carry):
            rows, new = [], []
            for u in range(GATHER_UNROLL):
                c = i * GATHER_UNROLL + u
                row = pl.multiple_of(idx_ref[0, 0, c] * SUBLANES, SUBLANES)
                rows.append(row)
                new.append(acc_ref[pl.ds(row, SUBLANES), :] + tile_ref[pl.ds(c, nchunk, stride=TILE_PITCH), :])
            for row, val in zip(rows, new):
                acc_ref[pl.ds(row, SUBLANES), :] = val
            return carry

        lax.fori_loop(0, cap // GATHER_UNROLL, body, 0)

    @pl.when((e == pl.num_programs(1) - 1) & (f == last_f))
    def _store_batch_row():
        c = pltpu.make_async_copy(acc_ref, o_hbm.at[b], sem.at[0])
        c.start()
        c.wait()


def _moe(x_slab, idx, val, g, w1, w3, w2, cap):
    batch, srows, _ = x_slab.shape
    nf = EXPERT_FF // MOE_TF
    return pl.pallas_call(
        functools.partial(_moe_kernel, cap=cap),
        grid=(batch, N_EXPERTS, nf),
        in_specs=[
            pl.BlockSpec((1, 1, cap), lambda b, e, f: (b * N_EXPERTS + e, 0, 0), memory_space=pltpu.SMEM),
            pl.BlockSpec(memory_space=pl.ANY),
            pl.BlockSpec((1, cap, 1), lambda b, e, f: (b * N_EXPERTS + e, 0, 0)),
            pl.BlockSpec((1, D_MODEL), lambda b, e, f: (0, 0)),
            pl.BlockSpec((1, D_MODEL, MOE_TF), lambda b, e, f: (e, 0, f)),
            pl.BlockSpec((1, D_MODEL, MOE_TF), lambda b, e, f: (e, 0, f)),
            pl.BlockSpec((1, MOE_TF, D_MODEL), lambda b, e, f: (e, f, 0)),
        ],
        out_specs=pl.BlockSpec(memory_space=pl.ANY),
        out_shape=jax.ShapeDtypeStruct(x_slab.shape, F32),
        scratch_shapes=[
            pltpu.VMEM((srows, LANES), F32),
            pltpu.VMEM((srows, LANES), F32),
            pltpu.VMEM((D_MODEL // LANES * TILE_PITCH, LANES), F32),
            pltpu.VMEM((cap, D_MODEL), BF16),
            pltpu.VMEM((cap, D_MODEL), F32),
            pltpu.SemaphoreType.DMA((2,)),
        ],
        compiler_params=_params(("arbitrary", "arbitrary", "arbitrary")),
        name="moe_ffn",
    )(idx, x_slab, val, g, w1, w3, w2)


def _ple_kernel(x_ref, p_ref, g_ref, wg_ref, wp_ref, gf_ref, o_ref, *, final):
    x = _from_slab(x_ref, TM)
    gate = _sigmoid(_dot(_rms(x, g_ref[...]).astype(BF16), wg_ref[...]))
    y = x + gate * _dot(p_ref[...].astype(BF16), wp_ref[...])
    if final:
        y = _rms(y, gf_ref[...])
    o_ref[...] = y


def _ple(x_slab, p2, g, wg, wp, gf, final):
    nslab = D_MODEL // LANES
    t = x_slab.shape[0] // nslab
    full = lambda shape: pl.BlockSpec(shape, lambda i: (0, 0))
    return pl.pallas_call(
        functools.partial(_ple_kernel, final=final),
        grid=(t // TM,),
        in_specs=[
            pl.BlockSpec((TM * nslab, LANES), lambda i: (i, 0)),
            pl.BlockSpec((TM, PLE_DIM), lambda i: (i, 0)),
            full((1, D_MODEL)), full((D_MODEL, D_MODEL)), full((PLE_DIM, D_MODEL)), full((1, D_MODEL)),
        ],
        out_specs=pl.BlockSpec((TM, D_MODEL), lambda i: (i, 0)),
        out_shape=jax.ShapeDtypeStruct((t, D_MODEL), F32),
        compiler_params=_params(("parallel",)),
        name="ple",
    )(x_slab, p2, g, wg, wp, gf)


def _rot_cols(w):
    q = MLA_ROPE // 4
    return jnp.concatenate([-w[:, q:2 * q], w[:, :q], -w[:, 3 * q:], w[:, 2 * q:3 * q]], axis=1)


def _prep_w_in(w):
    cuts = np.cumsum([NA_WIDTH, NA_WIDTH, NA_WIDTH, MLA_Q_RANK, MLA_KV_RANK, MLA_ROPE, D_MODEL])
    na_q, na_k, na_v, q_lat, kv_lat, k_rope, gate_a, gate_b = jnp.split(w, [int(c) for c in cuts], axis=1)
    z = lambda n: jnp.zeros((D_MODEL, n), w.dtype)
    out = jnp.concatenate([
        gate_a, gate_b, na_q, na_k, na_v,
        q_lat, z(Q_LAT_PAD - MLA_Q_RANK),
        kv_lat,
        z(MLA_NOPE), k_rope, z(HEAD_SLOT - MLA_NOPE - MLA_ROPE),
        z(MLA_NOPE), _rot_cols(k_rope), z(HEAD_SLOT - MLA_NOPE - MLA_ROPE),
    ], axis=1)
    assert out.shape[1] == N_PROJ
    return out.astype(BF16)


def _prep_mla_weights(wq_up, wkv_up):
    dqk = MLA_NOPE + MLA_ROPE
    wq = wq_up.reshape(MLA_Q_RANK, MLA_HEADS, dqk)
    nope, pe = wq[..., :MLA_NOPE], wq[..., MLA_NOPE:]
    pe_rot = _rot_cols(pe.reshape(MLA_Q_RANK * MLA_HEADS, MLA_ROPE)).reshape(MLA_Q_RANK, MLA_HEADS, MLA_ROPE)
    zq = lambda n: jnp.zeros((MLA_Q_RANK, MLA_HEADS, n), wq_up.dtype)
    tail = HEAD_SLOT - dqk
    wq_pad = jnp.concatenate([nope, pe, zq(tail)], axis=2).reshape(MLA_Q_RANK, MLA_HEADS * HEAD_SLOT)
    wq_rot = jnp.concatenate([zq(MLA_NOPE), pe_rot, zq(tail)], axis=2).reshape(MLA_Q_RANK, MLA_HEADS * HEAD_SLOT)
    rowpad = ((0, Q_LAT_PAD - MLA_Q_RANK), (0, 0))
    wq_pad = jnp.pad(wq_pad, rowpad).astype(BF16)
    wq_rot = jnp.pad(wq_rot, rowpad).astype(BF16)

    wkv = wkv_up.reshape(MLA_KV_RANK, MLA_HEADS, MLA_NOPE + MLA_V)
    k_nope, v = wkv[..., :MLA_NOPE], wkv[..., MLA_NOPE:]
    zk = jnp.zeros((MLA_KV_RANK, MLA_HEADS, HEAD_SLOT - MLA_NOPE), wkv_up.dtype)
    wk = jnp.concatenate([k_nope, zk], axis=2).reshape(MLA_KV_RANK, MLA_HEADS * HEAD_SLOT).astype(BF16)
    zv = jnp.zeros_like(v)
    even = (jnp.arange(MLA_HEADS) % 2 == 0)[None, :, None]
    wv = jnp.concatenate([jnp.where(even, v, zv), jnp.where(even, zv, v)], axis=2)
    wv = wv.reshape(MLA_KV_RANK, MLA_HEADS * HEAD_SLOT).astype(BF16)
    lane = np.arange(MLA_HEADS * HEAD_SLOT)
    head_even = (lane // HEAD_SLOT) % 2 == 0
    upper = (lane % HEAD_SLOT) >= MLA_V
    ones = jnp.asarray(np.where(head_even == upper, 1.0, 0.0).astype(np.float32))[None, :]
    return wq_pad, wq_rot, wk, wv, ones


def _rope_tables(seq):
    t = np.arange(seq)
    half = MLA_ROPE // 2
    freqs = 1.0 / (ROPE_BASE ** (jnp.arange(0, half, 2, dtype=F32) / half))

    def tab(pos):
        ang = jnp.asarray(pos, F32)[:, None] * freqs[None, :]
        return jnp.concatenate([jnp.cos(ang)] * 2, axis=1), jnp.concatenate([jnp.sin(ang)] * 2, axis=1)

    cr, sr = tab(t // GRID_W)
    cc, sc = tab(t % GRID_W)
    tail = HEAD_SLOT - MLA_NOPE - MLA_ROPE
    cos_t = jnp.concatenate([jnp.ones((seq, MLA_NOPE), F32), cr, cc, jnp.zeros((seq, tail), F32)], axis=1)
    sin_t = jnp.concatenate([jnp.zeros((seq, MLA_NOPE), F32), sr, sc, jnp.zeros((seq, tail), F32)], axis=1)
    return cos_t, sin_t


def kernel(x, p, norm_mix, w_in, na_rpb, mla_q_norm, mla_wq_up, mla_kv_norm, mla_wkv_up, w_na_o, w_mla_o, w_out,
           norm_moe, w_router, moe_w1, moe_w3, moe_w2, norm_ple, ple_gate_w, ple_w, norm_final):
    batch, seq, d = x.shape
    depth = w_in.shape[0]
    assert d == D_MODEL and seq % (GRID_W * NA_ROWS) == 0 and seq % TM == 0
    t = batch * seq
    cap = EC_CAPACITY * seq // N_EXPERTS
    assert TILE_PITCH == cap + SUBLANES and seq <= 64 * 64
    rows = seq // GRID_W
    cos_t, sin_t = _rope_tables(seq)
    row = lambda v: v.reshape(1, -1).astype(F32)

    x2 = x.reshape(t, d)
    for i in range(depth):
        proj = _in_proj(x2, row(norm_mix[i]), _prep_w_in(w_in[i]))
        ya = _na_attention(proj, _na_bias_tables(na_rpb[i], rows), batch, seq)
        wq, wqr, wk, wv, ones = _prep_mla_weights(mla_wq_up[i], mla_wkv_up[i])
        qn = jnp.pad(row(mla_q_norm[i]), ((0, 0), (0, Q_LAT_PAD - MLA_Q_RANK)))
        qp, kp, vp = _mla_prep(proj, qn, row(mla_kv_norm[i]), wq, wqr, wk, wv, ones, cos_t, sin_t, seq)
        yb = _flash(qp, kp, vp, batch, seq)
        x_slab, aff_t = _merge(x2, ya, yb, proj, w_na_o[i].astype(BF16), w_mla_o[i].astype(BF16),
                               w_out[i].astype(BF16), row(norm_moe[i]), _prep_router(w_router[i]), seq)

        picked = _select(aff_t, cap)
        val = (picked[..., 0] + picked[..., 1]) + picked[..., 2]
        idx = (picked[..., 3] * 64.0 + picked[..., 4]).astype(jnp.int32)
        x_slab = _moe(x_slab.reshape(batch, seq * (d // LANES), LANES),
                      idx.reshape(batch * N_EXPERTS, 1, cap), val.reshape(batch * N_EXPERTS, cap, 1),
                      row(norm_moe[i]), moe_w1[i].astype(BF16), moe_w3[i].astype(BF16), moe_w2[i].astype(BF16), cap)

        x2 = _ple(x_slab.reshape(t * (d // LANES), LANES), p[i].reshape(t, PLE_DIM), row(norm_ple[i]),
                  ple_gate_w[i].astype(BF16), ple_w[i].astype(BF16), row(norm_final), final=(i == depth - 1))
    return x2.reshape(batch, seq, d)
```

```python
import functools
import math

import jax
import jax.numpy as jnp
import numpy as np
from jax import lax
from jax.experimental import pallas as pl
from jax.experimental.pallas import tpu as pltpu

F32 = jnp.float32
BF16 = jnp.bfloat16

D_MODEL = 1024
GRID_W = 64
NA_HEADS = 8
NA_HEAD_DIM = 64
NA_WIN_ROWS = 8
NA_WIN_COLS = 16
NA_WIDTH = NA_HEADS * NA_HEAD_DIM
MLA_HEADS = 8
MLA_Q_RANK = 384
MLA_KV_RANK = 256
MLA_NOPE = 64
MLA_ROPE = 32
MLA_V = 64
MLA_WIDTH = MLA_HEADS * MLA_V
ROPE_BASE = 10000.0
N_EXPERTS = 16
EC_CAPACITY = 2
EXPERT_FF = 1024
PLE_DIM = 256
RMS_EPS = 1e-6

LANES = 128
SUBLANES = 8
HEAD_SLOT = 128
VMEM_LIMIT = 56 * 1024 * 1024

C_GATE_A = 0
C_GATE_B = 1024
C_NA_Q = 2048
C_NA_K = 2560
C_NA_V = 3072
C_Q_LAT = 3584
Q_LAT_PAD = 512
C_KV_LAT = 4096
C_K_ROPE = 4352
N_PROJ = 4608

TM = 512
NA_ROWS = 4
NA_KROWS = 12
FLASH_TQ = 256
FLASH_TK = 512
FLASH_SUBTILES = 2
MOE_TF = 512
MOE_SUB = 256
MOE_VMEM_LIMIT = 60 * 1024 * 1024
TILE_PITCH = 520
GATHER_UNROLL = 8
NEG_BIG = -1e30
SELECT_COLS = 16

MLA_QSCALE = float((MLA_NOPE + MLA_ROPE) ** -0.5 * math.log2(math.e))


def _dot(a, b):
    return jnp.dot(a, b, preferred_element_type=F32)


def _dot_nt(a, b):
    return lax.dot_general(a, b, (((1,), (1,)), ((), ())), preferred_element_type=F32)


def _rms(x, g, n=None):
    n = x.shape[-1] if n is None else n
    ms = jnp.sum(x * x, axis=-1, keepdims=True) * (1.0 / n)
    return (x * lax.rsqrt(ms + RMS_EPS)) * g


def _sigmoid(x):
    return 1.0 / (1.0 + jnp.exp(-x))


def _params(sem, vmem=VMEM_LIMIT):
    return pltpu.CompilerParams(dimension_semantics=sem, vmem_limit_bytes=vmem)


def _in_proj_kernel(x_ref, g_ref, w_ref, o_ref):
    h = _rms(x_ref[...], g_ref[...]).astype(BF16)
    for n in range(N_PROJ // 256):
        sl = slice(n * 256, (n + 1) * 256)
        o_ref[:, sl] = _dot(h, w_ref[:, sl]).astype(BF16)


def _in_proj(x2, g, w):
    t = x2.shape[0]
    return pl.pallas_call(
        _in_proj_kernel,
        grid=(t // TM,),
        in_specs=[
            pl.BlockSpec((TM, D_MODEL), lambda i: (i, 0)),
            pl.BlockSpec((1, D_MODEL), lambda i: (0, 0)),
            pl.BlockSpec((D_MODEL, N_PROJ), lambda i: (0, 0)),
        ],
        out_specs=pl.BlockSpec((TM, N_PROJ), lambda i: (i, 0)),
        out_shape=jax.ShapeDtypeStruct((t, N_PROJ), BF16),
        compiler_params=_params(("parallel",)),
        name="in_proj",
    )(x2, g, w)


def _na_kernel(q_ref, k_ref, v_ref, b_ref, o_ref):
    m = pl.program_id(1)
    rows = k_ref.shape[0] // GRID_W
    kr0 = jnp.clip(NA_ROWS * m - NA_ROWS, 0, rows - NA_KROWS)
    start = pl.multiple_of(kr0 * GRID_W, NA_ROWS * GRID_W)
    nk = NA_KROWS * GRID_W
    lane = lax.broadcasted_iota(jnp.int32, (q_ref.shape[0], LANES), 1)
    scale = NA_HEAD_DIM ** -0.5

    def scores(h):
        pair = slice((h // 2) * LANES, (h // 2 + 1) * LANES)
        q = q_ref[:, pair]
        lo = (h % 2) * NA_HEAD_DIM
        qh = jnp.where((lane >= lo) & (lane < lo + NA_HEAD_DIM), q, jnp.zeros_like(q)) * scale
        return _dot_nt(qh, k_ref[pl.ds(start, nk), pair]) + b_ref[0, h]

    outs = []
    s_next = scores(0)
    for h in range(NA_HEADS):
        s = s_next
        if h + 1 < NA_HEADS:
            s_next = scores(h + 1)
        pair = slice((h // 2) * LANES, (h // 2 + 1) * LANES)
        p = jnp.exp(s - jnp.max(s, axis=1, keepdims=True))
        l = jnp.sum(p, axis=1, keepdims=True)
        outs.append(_dot(p.astype(BF16), v_ref[pl.ds(start, nk), pair]) / l)
        if h % 2 == 1:
            o_ref[:, pair] = jnp.where(lane < NA_HEAD_DIM, outs[h - 1], outs[h]).astype(BF16)


def _na_attention(proj, bias, batch, seq):
    rows = seq // GRID_W
    nblk = rows // NA_ROWS
    tq = NA_ROWS * GRID_W

    def variant(m):
        return jnp.where(m == 0, 0, jnp.where(m == nblk - 1, 2, 1))

    return pl.pallas_call(
        _na_kernel,
        grid=(batch, nblk),
        in_specs=[
            pl.BlockSpec((tq, NA_WIDTH), lambda b, m: (b * nblk + m, C_NA_Q // NA_WIDTH)),
            pl.BlockSpec((seq, NA_WIDTH), lambda b, m: (b, C_NA_K // NA_WIDTH)),
            pl.BlockSpec((seq, NA_WIDTH), lambda b, m: (b, C_NA_V // NA_WIDTH)),
            pl.BlockSpec((1, NA_HEADS, tq, NA_KROWS * GRID_W), lambda b, m: (variant(m), 0, 0, 0)),
        ],
        out_specs=pl.BlockSpec((tq, NA_WIDTH), lambda b, m: (b * nblk + m, 0)),
        out_shape=jax.ShapeDtypeStruct((batch * seq, NA_WIDTH), BF16),
        compiler_params=_params(("parallel", "arbitrary")),
        name="na_attention",
    )(proj, proj, proj, bias)


def _na_bias_tables(rpb, rows):
    nblk = rows // NA_ROWS
    qr = np.arange(NA_ROWS)
    kk = np.arange(NA_KROWS)
    row_sel = np.zeros((3, NA_ROWS, NA_KROWS, 2 * NA_WIN_ROWS - 1), np.float32)
    row_ok = np.zeros((3, NA_ROWS, NA_KROWS), bool)
    for vi, m in enumerate((0, 1, nblk - 1)):
        r = NA_ROWS * m + qr
        r0 = np.clip(r - NA_WIN_ROWS // 2, 0, rows - NA_WIN_ROWS)
        kr0 = np.clip(NA_ROWS * m - NA_ROWS, 0, rows - NA_KROWS)
        key_row = kr0 + kk
        ok = (key_row[None, :] >= r0[:, None]) & (key_row[None, :] < r0[:, None] + NA_WIN_ROWS)
        a = np.clip(key_row[None, :] - r[:, None] + NA_WIN_ROWS - 1, 0, 2 * NA_WIN_ROWS - 2)
        row_ok[vi] = ok
        row_sel[vi, qr[:, None], kk[None, :], a] = 1.0
    c = np.arange(GRID_W)
    cs = np.clip(c - NA_WIN_COLS // 2, 0, GRID_W - NA_WIN_COLS)
    col_ok = (c[None, :] >= cs[:, None]) & (c[None, :] < cs[:, None] + NA_WIN_COLS)
    bc = np.clip(c[None, :] - c[:, None] + NA_WIN_COLS - 1, 0, 2 * NA_WIN_COLS - 2)
    col_sel = np.zeros((GRID_W, GRID_W, 2 * NA_WIN_COLS - 1), np.float32)
    col_sel[c[:, None], c[None, :], bc] = 1.0
    bias = jnp.einsum("vqka,hab,cjb->vhqckj", jnp.asarray(row_sel), rpb.astype(F32), jnp.asarray(col_sel),
                      precision=lax.Precision.HIGHEST)
    ok = row_ok[:, None, :, None, :, None] & col_ok[None, None, None, :, None, :]
    bias = jnp.where(jnp.asarray(ok), bias, NEG_BIG)
    return bias.reshape(3, NA_HEADS, NA_ROWS * GRID_W, NA_KROWS * GRID_W)


def _mla_prep_kernel(ql_ref, kvl_ref, kr_ref, qn_ref, kvn_ref, wq_ref, wqr_ref, wk_ref, wv_ref,
                     ones_ref, cos_ref, sin_ref, q_out, k_out, v_out):
    cos = cos_ref[...]
    sin = sin_ref[...]
    cos8 = jnp.concatenate([cos] * MLA_HEADS, axis=1)
    sin8 = jnp.concatenate([sin] * MLA_HEADS, axis=1)
    hq = _rms(ql_ref[...].astype(F32), qn_ref[...], MLA_Q_RANK).astype(BF16)
    q = _dot(hq, wq_ref[...]) * cos8 + _dot(hq, wqr_ref[...]) * sin8
    q_out[...] = (q * MLA_QSCALE).astype(BF16)
    hkv = _rms(kvl_ref[...].astype(F32), kvn_ref[...]).astype(BF16)
    kr = kr_ref[...].astype(F32)
    kpe = kr[:, :HEAD_SLOT] * cos + kr[:, HEAD_SLOT:] * sin
    k = _dot(hkv, wk_ref[...]) + jnp.concatenate([kpe] * MLA_HEADS, axis=1)
    k_out[0] = k.T.astype(BF16)
    v_out[...] = (_dot(hkv, wv_ref[...]) + ones_ref[...]).astype(BF16)


def _mla_prep(proj, qn, kvn, wq, wqr, wk, wv, ones, cos_t, sin_t, seq):
    t = proj.shape[0]
    width = MLA_HEADS * HEAD_SLOT
    sblk = seq // TM
    full = lambda shape: pl.BlockSpec(shape, lambda i: (0, 0))
    out = jax.ShapeDtypeStruct((t, width), BF16)
    return pl.pallas_call(
        _mla_prep_kernel,
        grid=(t // TM,),
        in_specs=[
            pl.BlockSpec((TM, Q_LAT_PAD), lambda i: (i, C_Q_LAT // Q_LAT_PAD)),
            pl.BlockSpec((TM, MLA_KV_RANK), lambda i: (i, C_KV_LAT // MLA_KV_RANK)),
            pl.BlockSpec((TM, 2 * HEAD_SLOT), lambda i: (i, C_K_ROPE // (2 * HEAD_SLOT))),
            full((1, Q_LAT_PAD)), full((1, MLA_KV_RANK)),
            full((Q_LAT_PAD, width)), full((Q_LAT_PAD, width)),
            full((MLA_KV_RANK, width)), full((MLA_KV_RANK, width)),
            full((1, width)),
            pl.BlockSpec((TM, HEAD_SLOT), lambda i: (i % sblk, 0)),
            pl.BlockSpec((TM, HEAD_SLOT), lambda i: (i % sblk, 0)),
        ],
        out_specs=[pl.BlockSpec((TM, width), lambda i: (i, 0)),
                   pl.BlockSpec((1, width, TM), lambda i: (i // sblk, 0, i % sblk)),
                   pl.BlockSpec((TM, width), lambda i: (i, 0))],
        out_shape=[out, jax.ShapeDtypeStruct((t // seq, width, seq), BF16), out],
        compiler_params=_params(("parallel",)),
        name="mla_prep",
    )(proj, proj, proj, qn, kvn, wq, wqr, wk, wv, ones, cos_t, sin_t)


def _flash_kernel(q_ref, kt_ref, v_ref, o_ref):
    seq = v_ref.shape[0]
    nsub = q_ref.shape[0] // FLASH_TQ
    hslices = [slice(h * HEAD_SLOT, (h + 1) * HEAD_SLOT) for h in range(2)]
    streams = [(qi, h) for qi in range(nsub) for h in range(2)]
    qs = {(qi, h): q_ref[qi * FLASH_TQ:(qi + 1) * FLASH_TQ, hslices[h]] for qi, h in streams}
    ms = {st: jnp.full((FLASH_TQ, 1), NEG_BIG, F32) for st in streams}
    accs = {st: jnp.zeros((FLASH_TQ, HEAD_SLOT), F32) for st in streams}
    items = [(j, st) for j in range(seq // FLASH_TK) for st in streams]

    def scores(item):
        j, st = item
        return _dot(qs[st], kt_ref[0, hslices[st[1]], j * FLASH_TK:(j + 1) * FLASH_TK])

    s_next = scores(items[0])
    for n, (j, st) in enumerate(items):
        s = s_next
        if n + 1 < len(items):
            s_next = scores(items[n + 1])
        v = v_ref[j * FLASH_TK:(j + 1) * FLASH_TK, hslices[st[1]]]
        m_new = jnp.maximum(ms[st], jnp.max(s, axis=1, keepdims=True))
        p = jnp.exp2((s - m_new).astype(BF16))
        alpha = jnp.exp2(ms[st] - m_new)
        accs[st] = alpha * accs[st] + _dot(p, v)
        ms[st] = m_new
    lane = lax.broadcasted_iota(jnp.int32, (FLASH_TQ, HEAD_SLOT), 1)
    for qi in range(nsub):
        o0, o1 = [accs[(qi, h)] / pltpu.roll(accs[(qi, h)], MLA_V, axis=1) for h in range(2)]
        o_ref[qi * FLASH_TQ:(qi + 1) * FLASH_TQ, :] = jnp.where(lane < MLA_V, o0, o1).astype(BF16)


def _flash(qp, kp, vp, batch, seq):
    tq = FLASH_TQ * FLASH_SUBTILES
    nq = seq // tq
    return pl.pallas_call(
        _flash_kernel,
        grid=(batch, MLA_HEADS // 2, nq),
        in_specs=[
            pl.BlockSpec((tq, 2 * HEAD_SLOT), lambda b, hp, i: (b * nq + i, hp)),
            pl.BlockSpec((1, 2 * HEAD_SLOT, seq), lambda b, hp, i: (b, hp, 0)),
            pl.BlockSpec((seq, 2 * HEAD_SLOT), lambda b, hp, i: (b, hp)),
        ],
        out_specs=pl.BlockSpec((tq, LANES), lambda b, hp, i: (b * nq + i, hp)),
        out_shape=jax.ShapeDtypeStruct((batch * seq, MLA_WIDTH), BF16),
        compiler_params=_params(("parallel", "parallel", "arbitrary")),
        name="mla_flash",
    )(qp, kp, vp)


def _to_slab(o_ref, x):
    for j in range(D_MODEL // LANES):
        o_ref[pl.ds(j, x.shape[0], stride=D_MODEL // LANES), :] = x[:, j * LANES:(j + 1) * LANES]


def _from_slab(x_ref, rows):
    n = D_MODEL // LANES
    return jnp.concatenate([x_ref[pl.ds(j, rows, stride=n), :] for j in range(n)], axis=1)


def _merge_kernel(x_ref, ya_ref, yb_ref, ga_ref, gb_ref, wa_ref, wb_ref, wo_ref, gm_ref, wr_ref, o_ref, aff_ref):
    ya = _dot(ya_ref[...], wa_ref[...])
    yb = _dot(yb_ref[...], wb_ref[...])
    merged = _sigmoid(ga_ref[...].astype(F32)) * ya + _sigmoid(gb_ref[...].astype(F32)) * yb
    x = x_ref[...] + _dot(merged.astype(BF16), wo_ref[...])
    _to_slab(o_ref, x)
    h = _rms(x, gm_ref[...])
    h_hi = h.astype(BF16)
    h_lo = (h - h_hi.astype(F32)).astype(BF16)
    r = _dot(h_hi, wr_ref[...])
    logits = (r[:, :LANES] + r[:, LANES:]) + _dot(h_lo, wr_ref[:, :LANES])
    lane = lax.broadcasted_iota(jnp.int32, logits.shape, 1)
    logits = jnp.where(lane < N_EXPERTS, logits, NEG_BIG)
    e = jnp.exp(logits - jnp.max(logits, axis=1, keepdims=True))
    aff = e / jnp.sum(e, axis=1, keepdims=True)
    aff_ref[0] = aff.T[:N_EXPERTS, :]


def _merge(x2, ya, yb, proj, wa, wb, wo, gm, wr, seq):
    t = x2.shape[0]
    sblk = seq // TM
    nslab = D_MODEL // LANES
    full = lambda shape: pl.BlockSpec(shape, lambda i: (0, 0))
    return pl.pallas_call(
        _merge_kernel,
        grid=(t // TM,),
        in_specs=[
            pl.BlockSpec((TM, D_MODEL), lambda i: (i, 0)),
            pl.BlockSpec((TM, NA_WIDTH), lambda i: (i, 0)),
            pl.BlockSpec((TM, MLA_WIDTH), lambda i: (i, 0)),
            pl.BlockSpec((TM, D_MODEL), lambda i: (i, C_GATE_A // D_MODEL)),
            pl.BlockSpec((TM, D_MODEL), lambda i: (i, C_GATE_B // D_MODEL)),
            full((NA_WIDTH, D_MODEL)), full((MLA_WIDTH, D_MODEL)), full((D_MODEL, D_MODEL)),
            full((1, D_MODEL)), full((D_MODEL, 2 * LANES)),
        ],
        out_specs=[pl.BlockSpec((TM * nslab, LANES), lambda i: (i, 0)),
                   pl.BlockSpec((1, N_EXPERTS, TM), lambda i: (i // sblk, 0, i % sblk))],
        out_shape=[jax.ShapeDtypeStruct((t * nslab, LANES), F32),
                   jax.ShapeDtypeStruct((t // seq, N_EXPERTS, seq), F32)],
        compiler_params=_params(("parallel",)),
        name="merge",
    )(x2, ya, yb, proj, proj, wa, wb, wo, gm, wr)


def _prep_router(w):
    hi = w.astype(BF16)
    lo = (w - hi.astype(F32)).astype(BF16)
    z = jnp.zeros((D_MODEL, LANES - N_EXPERTS), BF16)
    return jnp.concatenate([hi, z, lo, z], axis=1)


def _cumsum_lanes(x):
    n = x.shape[1]
    lane = lax.broadcasted_iota(jnp.int32, x.shape, 1)
    sh = 1
    while sh < n:
        x = x + jnp.where(lane >= sh, pltpu.roll(x, sh, axis=1), 0.0)
        sh *= 2
    return x


def _select_kernel(aff_ref, val_ref, idx_ref, *, cap):
    aff = aff_ref[0]
    seq = aff.shape[1]
    bits = pltpu.bitcast(aff, jnp.int32)
    thr = jnp.zeros((N_EXPERTS, 1), jnp.int32)
    for bit in range(30, -1, -1):
        cand = thr | (1 << bit)
        cnt = jnp.sum((bits >= cand).astype(jnp.int32), axis=1, keepdims=True)
        thr = jnp.where(cnt >= cap, cand, thr)
    gt = bits > thr
    eq = bits == thr
    need = cap - jnp.sum(gt.astype(F32), axis=1, keepdims=True)
    sel = gt | (eq & (_cumsum_lanes(eq.astype(F32)) <= need))
    slot = jnp.where(sel, _cumsum_lanes(sel.astype(F32)) - 1.0, -1.0)

    hi = aff.astype(BF16).astype(F32)
    r1 = aff - hi
    mid = r1.astype(BF16).astype(F32)
    lo = (r1 - mid).astype(BF16).astype(F32)
    tok = lax.broadcasted_iota(jnp.int32, (1, seq), 1)
    t_hi = (tok >> 6).astype(F32)
    t_lo = (tok & 63).astype(F32)
    pad = jnp.zeros((SELECT_COLS - 5, seq), F32)
    cblk = 128
    for e in range(N_EXPERTS):
        rhs = jnp.concatenate([hi[e:e + 1], mid[e:e + 1], lo[e:e + 1], t_hi, t_lo, pad], axis=0).astype(BF16)
        slot_e = slot[e:e + 1, :]
        for cc in range(cap // cblk):
            ci = (lax.broadcasted_iota(jnp.int32, (cblk, 1), 0) + cc * cblk).astype(F32)
            onehot = jnp.where(slot_e == ci, 1.0, 0.0).astype(BF16)
            r = _dot_nt(onehot, rhs)
            cs = slice(cc * cblk, (cc + 1) * cblk)
            val_ref[0, e, cs, :] = (r[:, 0:1] + r[:, 1:2]) + r[:, 2:3]
            rt = r.T
            idx_ref[0, e, :, cs] = (rt[3:4, :] * 64.0 + rt[4:5, :]).astype(jnp.int32)


def _select(aff_t, cap):
    batch, _, seq = aff_t.shape
    return pl.pallas_call(
        functools.partial(_select_kernel, cap=cap),
        grid=(batch,),
        in_specs=[pl.BlockSpec((1, N_EXPERTS, seq), lambda b: (b, 0, 0))],
        out_specs=[pl.BlockSpec((1, N_EXPERTS, cap, 1), lambda b: (b, 0, 0, 0)),
                   pl.BlockSpec((1, N_EXPERTS, 1, cap), lambda b: (b, 0, 0, 0))],
        out_shape=[jax.ShapeDtypeStruct((batch, N_EXPERTS, cap, 1), F32),
                   jax.ShapeDtypeStruct((batch, N_EXPERTS, 1, cap), jnp.int32)],
        compiler_params=_params(("parallel",)),
        name="moe_select",
    )(aff_t)


def _gather_tokens(idx_ref, xs_ref, tile_ref, tokens):
    n = D_MODEL // LANES
    for c in tokens:
        row = pl.multiple_of(idx_ref[0, 0, c] * SUBLANES, SUBLANES)
        tile_ref[pl.ds(c, n, stride=TILE_PITCH), :] = xs_ref[pl.ds(row, SUBLANES), :]


def _scatter_tokens(idx_ref, tile_ref, acc_ref, tokens):
    n = D_MODEL // LANES
    rows, new = [], []
    for c in tokens:
        row = pl.multiple_of(idx_ref[0, 0, c] * SUBLANES, SUBLANES)
        rows.append(row)
        new.append(acc_ref[pl.ds(row, SUBLANES), :] + tile_ref[pl.ds(c, n, stride=TILE_PITCH), :])
    for row, val in zip(rows, new):
        acc_ref[pl.ds(row, SUBLANES), :] = val


def _tile_to_rows(tile_ref, cap):
    n = D_MODEL // LANES
    return jnp.concatenate([tile_ref[pl.ds(j * TILE_PITCH, cap), :] for j in range(n)], axis=1)


def _expert_ffn_half(xg, w1_ref, w3_ref, w2_ref, side_work):
    part = None
    nsub = MOE_TF // MOE_SUB
    for k in range(nsub):
        cs = slice(k * MOE_SUB, (k + 1) * MOE_SUB)
        h1 = _dot(xg, w1_ref[0, :, cs])
        h3 = _dot(xg, w3_ref[0, :, cs])
        act = ((h1 * _sigmoid(h1)) * h3).astype(BF16)
        d = _dot(act, w2_ref[0, cs, :])
        part = d if part is None else part + d
        side_work(k, nsub)
    return part


def _moe_kernel(idxp_ref, idxc_ref, idxn_ref, x_hbm, val_ref, g_ref, w1_ref, w3_ref, w2_ref, o_hbm,
                xs_ref, acc_ref, xtile_ref, ytile_ref, xg_ref, y_ref, sem, *, cap):
    b = pl.program_id(0)
    e = pl.program_id(1)
    f = pl.program_id(2)
    last_e = pl.num_programs(1) - 1
    nchunk = D_MODEL // LANES
    groups = cap // GATHER_UNROLL

    @pl.when((e == 0) & (f == 0))
    def _start_batch_row():
        c0 = pltpu.make_async_copy(x_hbm.at[b], xs_ref, sem.at[0])
        c1 = pltpu.make_async_copy(x_hbm.at[b], acc_ref, sem.at[1])
        c0.start()
        c1.start()
        ytile_ref[...] = jnp.zeros_like(ytile_ref)
        c0.wait()
        c1.wait()

        def body(i, carry):
            _gather_tokens(idxc_ref, xs_ref, xtile_ref, [i * GATHER_UNROLL + u for u in range(GATHER_UNROLL)])
            return carry

        lax.fori_loop(0, groups, body, 0)
        xg_ref[0] = _rms(_tile_to_rows(xtile_ref, cap), g_ref[...]).astype(BF16)

    @pl.when(f == 0)
    def _tile0_and_scatter_previous():
        def side(k, nsub):
            lo, hi = k * groups // nsub, (k + 1) * groups // nsub
            for grp in range(lo, hi):
                _scatter_tokens(idxp_ref, ytile_ref, acc_ref, range(grp * GATHER_UNROLL, (grp + 1) * GATHER_UNROLL))

        y_ref[...] = _expert_ffn_half(xg_ref[e & 1], w1_ref, w3_ref, w2_ref, side)

    @pl.when(f == 1)
    def _tile1_and_gather_next():
        def side(k, nsub):
            _gather_tokens(idxn_ref, xs_ref, xtile_ref, range(k * cap // nsub, (k + 1) * cap // nsub))

        y = (y_ref[...] + _expert_ffn_half(xg_ref[e & 1], w1_ref, w3_ref, w2_ref, side)) * val_ref[0]
        for j in range(nchunk):
            ytile_ref[pl.ds(j * TILE_PITCH, cap), :] = y[:, j * LANES:(j + 1) * LANES]
        xg_ref[(e + 1) & 1] = _rms(_tile_to_rows(xtile_ref, cap), g_ref[...]).astype(BF16)

    @pl.when((e == last_e) & (f == 1))
    def _finish_batch_row():
        def body(i, carry):
            _scatter_tokens(idxc_ref, ytile_ref, acc_ref, [i * GATHER_UNROLL + u for u in range(GATHER_UNROLL)])
            return carry

        lax.fori_loop(0, groups, body, 0)
        c = pltpu.make_async_copy(acc_ref, o_hbm.at[b], sem.at[0])
        c.start()
        c.wait()


def _moe(x_slab, idx, val, g, w1, w3, w2, cap):
    batch, srows, _ = x_slab.shape
    nf = EXPERT_FF // MOE_TF
    assert nf == 2 and cap % GATHER_UNROLL == 0
    last = N_EXPERTS - 1
    idx_spec = lambda shift: pl.BlockSpec(
        (1, 1, cap), lambda b, e, f: (b * N_EXPERTS + jnp.clip(e + shift, 0, last), 0, 0), memory_space=pltpu.SMEM)
    tile = pltpu.VMEM((D_MODEL // LANES * TILE_PITCH, LANES), F32)
    return pl.pallas_call(
        functools.partial(_moe_kernel, cap=cap),
        grid=(batch, N_EXPERTS, nf),
        in_specs=[
            idx_spec(-1), idx_spec(0), idx_spec(1),
            pl.BlockSpec(memory_space=pl.ANY),
            pl.BlockSpec((1, cap, 1), lambda b, e, f: (b * N_EXPERTS + e, 0, 0)),
            pl.BlockSpec((1, D_MODEL), lambda b, e, f: (0, 0)),
            pl.BlockSpec((1, D_MODEL, MOE_TF), lambda b, e, f: (e, 0, f)),
            pl.BlockSpec((1, D_MODEL, MOE_TF), lambda b, e, f: (e, 0, f)),
            pl.BlockSpec((1, MOE_TF, D_MODEL), lambda b, e, f: (e, f, 0)),
        ],
        out_specs=pl.BlockSpec(memory_space=pl.ANY),
        out_shape=jax.ShapeDtypeStruct(x_slab.shape, F32),
        scratch_shapes=[
            pltpu.VMEM((srows, LANES), F32),
            pltpu.VMEM((srows, LANES), F32),
            tile, tile,
            pltpu.VMEM((2, cap, D_MODEL), BF16),
            pltpu.VMEM((cap, D_MODEL), F32),
            pltpu.SemaphoreType.DMA((2,)),
        ],
        compiler_params=_params(("arbitrary", "arbitrary", "arbitrary"), MOE_VMEM_LIMIT),
        name="moe_ffn",
    )(idx, idx, idx, x_slab, val, g, w1, w3, w2)


def _ple_kernel(x_ref, p_ref, g_ref, wg_ref, wp_ref, gf_ref, o_ref, *, final):
    x = _from_slab(x_ref, TM)
    gate = _sigmoid(_dot(_rms(x, g_ref[...]).astype(BF16), wg_ref[...]))
    y = x + gate * _dot(p_ref[...].astype(BF16), wp_ref[...])
    if final:
        y = _rms(y, gf_ref[...])
    o_ref[...] = y


def _ple(x_slab, p2, g, wg, wp, gf, final):
    nslab = D_MODEL // LANES
    t = x_slab.shape[0] // nslab
    full = lambda shape: pl.BlockSpec(shape, lambda i: (0, 0))
    return pl.pallas_call(
        functools.partial(_ple_kernel, final=final),
        grid=(t // TM,),
        in_specs=[
            pl.BlockSpec((TM * nslab, LANES), lambda i: (i, 0)),
            pl.BlockSpec((TM, PLE_DIM), lambda i: (i, 0)),
            full((1, D_MODEL)), full((D_MODEL, D_MODEL)), full((PLE_DIM, D_MODEL)), full((1, D_MODEL)),
        ],
        out_specs=pl.BlockSpec((TM, D_MODEL), lambda i: (i, 0)),
        out_shape=jax.ShapeDtypeStruct((t, D_MODEL), F32),
        compiler_params=_params(("parallel",)),
        name="ple",
    )(x_slab, p2, g, wg, wp, gf)


def _cast_kernel(w_ref, o_ref):
    o_ref[...] = w_ref[...].astype(BF16)


def _expert_weights_bf16(w, layer):
    _, e, r, c = w.shape
    return pl.pallas_call(
        _cast_kernel,
        grid=(e,),
        in_specs=[pl.BlockSpec((1, 1, r, c), lambda i: (layer, i, 0, 0))],
        out_specs=pl.BlockSpec((1, 1, r, c), lambda i: (0, i, 0, 0)),
        out_shape=jax.ShapeDtypeStruct((1, e, r, c), BF16),
        compiler_params=_params(("parallel",)),
        name="expert_weight_cast",
    )(w)[0]
def _rot_cols(w):
    q = MLA_ROPE // 4
    return jnp.concatenate([-w[:, q:2 * q], w[:, :q], -w[:, 3 * q:], w[:, 2 * q:3 * q]], axis=1)


def _prep_w_in(w):
    cuts = np.cumsum([NA_WIDTH, NA_WIDTH, NA_WIDTH, MLA_Q_RANK, MLA_KV_RANK, MLA_ROPE, D_MODEL])
    na_q, na_k, na_v, q_lat, kv_lat, k_rope, gate_a, gate_b = jnp.split(w, [int(c) for c in cuts], axis=1)
    z = lambda n: jnp.zeros((D_MODEL, n), w.dtype)
    out = jnp.concatenate([
        gate_a, gate_b, na_q, na_k, na_v,
        q_lat, z(Q_LAT_PAD - MLA_Q_RANK),
        kv_lat,
        z(MLA_NOPE), k_rope, z(HEAD_SLOT - MLA_NOPE - MLA_ROPE),
        z(MLA_NOPE), _rot_cols(k_rope), z(HEAD_SLOT - MLA_NOPE - MLA_ROPE),
    ], axis=1)
    assert out.shape[1] == N_PROJ
    return out.astype(BF16)


def _prep_mla_weights(wq_up, wkv_up):
    dqk = MLA_NOPE + MLA_ROPE
    wq = wq_up.reshape(MLA_Q_RANK, MLA_HEADS, dqk)
    nope, pe = wq[..., :MLA_NOPE], wq[..., MLA_NOPE:]
    pe_rot = _rot_cols(pe.reshape(MLA_Q_RANK * MLA_HEADS, MLA_ROPE)).reshape(MLA_Q_RANK, MLA_HEADS, MLA_ROPE)
    zq = lambda n: jnp.zeros((MLA_Q_RANK, MLA_HEADS, n), wq_up.dtype)
    tail = HEAD_SLOT - dqk
    wq_pad = jnp.concatenate([nope, pe, zq(tail)], axis=2).reshape(MLA_Q_RANK, MLA_HEADS * HEAD_SLOT)
    wq_rot = jnp.concatenate([zq(MLA_NOPE), pe_rot, zq(tail)], axis=2).reshape(MLA_Q_RANK, MLA_HEADS * HEAD_SLOT)
    rowpad = ((0, Q_LAT_PAD - MLA_Q_RANK), (0, 0))
    wq_pad = jnp.pad(wq_pad, rowpad).astype(BF16)
    wq_rot = jnp.pad(wq_rot, rowpad).astype(BF16)

    wkv = wkv_up.reshape(MLA_KV_RANK, MLA_HEADS, MLA_NOPE + MLA_V)
    k_nope, v = wkv[..., :MLA_NOPE], wkv[..., MLA_NOPE:]
    zk = jnp.zeros((MLA_KV_RANK, MLA_HEADS, HEAD_SLOT - MLA_NOPE), wkv_up.dtype)
    wk = jnp.concatenate([k_nope, zk], axis=2).reshape(MLA_KV_RANK, MLA_HEADS * HEAD_SLOT).astype(BF16)
    zv = jnp.zeros_like(v)
    even = (jnp.arange(MLA_HEADS) % 2 == 0)[None, :, None]
    wv = jnp.concatenate([jnp.where(even, v, zv), jnp.where(even, zv, v)], axis=2)
    wv = wv.reshape(MLA_KV_RANK, MLA_HEADS * HEAD_SLOT).astype(BF16)
    lane = np.arange(MLA_HEADS * HEAD_SLOT)
    head_even = (lane // HEAD_SLOT) % 2 == 0
    upper = (lane % HEAD_SLOT) >= MLA_V
    ones = jnp.asarray(np.where(head_even == upper, 1.0, 0.0).astype(np.float32))[None, :]
    return wq_pad, wq_rot, wk, wv, ones


def _rope_tables(seq):
    t = np.arange(seq)
    half = MLA_ROPE // 2
    freqs = 1.0 / (ROPE_BASE ** (jnp.arange(0, half, 2, dtype=F32) / half))

    def tab(pos):
        ang = jnp.asarray(pos, F32)[:, None] * freqs[None, :]
        return jnp.concatenate([jnp.cos(ang)] * 2, axis=1), jnp.concatenate([jnp.sin(ang)] * 2, axis=1)

    cr, sr = tab(t // GRID_W)
    cc, sc = tab(t % GRID_W)
    tail = HEAD_SLOT - MLA_NOPE - MLA_ROPE
    cos_t = jnp.concatenate([jnp.ones((seq, MLA_NOPE), F32), cr, cc, jnp.zeros((seq, tail), F32)], axis=1)
    sin_t = jnp.concatenate([jnp.zeros((seq, MLA_NOPE), F32), sr, sc, jnp.zeros((seq, tail), F32)], axis=1)
    return cos_t, sin_t


def kernel(x, p, norm_mix, w_in, na_rpb, mla_q_norm, mla_wq_up, mla_kv_norm, mla_wkv_up, w_na_o, w_mla_o, w_out,
           norm_moe, w_router, moe_w1, moe_w3, moe_w2, norm_ple, ple_gate_w, ple_w, norm_final):
    batch, seq, d = x.shape
    depth = w_in.shape[0]
    assert d == D_MODEL and seq % (GRID_W * NA_ROWS) == 0 and seq % TM == 0
    t = batch * seq
    cap = EC_CAPACITY * seq // N_EXPERTS
    assert TILE_PITCH == cap + SUBLANES and seq <= 64 * 64
    rows = seq // GRID_W
    cos_t, sin_t = _rope_tables(seq)
    row = lambda v: v.reshape(1, -1).astype(F32)

    x2 = x.reshape(t, d)
    for i in range(depth):
        proj = _in_proj(x2, row(norm_mix[i]), _prep_w_in(w_in[i]))
        ya = _na_attention(proj, _na_bias_tables(na_rpb[i], rows), batch, seq)
        wq, wqr, wk, wv, ones = _prep_mla_weights(mla_wq_up[i], mla_wkv_up[i])
        qn = jnp.pad(row(mla_q_norm[i]), ((0, 0), (0, Q_LAT_PAD - MLA_Q_RANK)))
        qp, kp, vp = _mla_prep(proj, qn, row(mla_kv_norm[i]), wq, wqr, wk, wv, ones, cos_t, sin_t, seq)
        yb = _flash(qp, kp, vp, batch, seq)
        x_slab, aff_t = _merge(x2, ya, yb, proj, w_na_o[i].astype(BF16), w_mla_o[i].astype(BF16),
                               w_out[i].astype(BF16), row(norm_moe[i]), _prep_router(w_router[i]), seq)

        val, idx = _select(aff_t, cap)
        x_slab = _moe(x_slab.reshape(batch, seq * (d // LANES), LANES),
                      idx.reshape(batch * N_EXPERTS, 1, cap), val.reshape(batch * N_EXPERTS, cap, 1),
                      row(norm_moe[i]), _expert_weights_bf16(moe_w1, i), _expert_weights_bf16(moe_w3, i),
                      _expert_weights_bf16(moe_w2, i), cap)

        x2 = _ple(x_slab.reshape(t * (d // LANES), LANES), p[i].reshape(t, PLE_DIM), row(norm_ple[i]),
                  ple_gate_w[i].astype(BF16), ple_w[i].astype(BF16), row(norm_final), final=(i == depth - 1))
    return x2.reshape(batch, seq, d)
```

```python
import functools
import math

import jax
import jax.numpy as jnp
import numpy as np
from jax import lax
from jax.experimental import pallas as pl
from jax.experimental.pallas import tpu as pltpu

F32 = jnp.float32
BF16 = jnp.bfloat16

D_MODEL = 1024
GRID_W = 64
NA_HEADS = 8
NA_HEAD_DIM = 64
NA_WIN_ROWS = 8
NA_WIN_COLS = 16
NA_WIDTH = NA_HEADS * NA_HEAD_DIM
MLA_HEADS = 8
MLA_Q_RANK = 384
MLA_KV_RANK = 256
MLA_NOPE = 64
MLA_ROPE = 32
MLA_V = 64
MLA_WIDTH = MLA_HEADS * MLA_V
ROPE_BASE = 10000.0
N_EXPERTS = 16
EC_CAPACITY = 2
EXPERT_FF = 1024
PLE_DIM = 256
RMS_EPS = 1e-6

LANES = 128
SUBLANES = 8
HEAD_SLOT = 128
VMEM_LIMIT = 56 * 1024 * 1024

C_GATE_A = 0
C_GATE_B = 1024
C_NA_Q = 2048
C_NA_K = 2560
C_NA_V = 3072
C_Q_LAT = 3584
Q_LAT_PAD = 512
C_KV_LAT = 4096
C_K_ROPE = 4352
N_PROJ = 4608

TM = 512
NA_ROWS = 4
NA_KROWS = 12
FLASH_TQ = 256
FLASH_TK = 512
FLASH_SUBTILES = 2
MOE_SUB = 256
MOE_VMEM_LIMIT = 60 * 1024 * 1024
TILE_PITCH = 520
GATHER_UNROLL = 8
NEG_BIG = -1e30
SELECT_COLS = 16

MLA_QSCALE = float((MLA_NOPE + MLA_ROPE) ** -0.5 * math.log2(math.e))


def _dot(a, b):
    return jnp.dot(a, b, preferred_element_type=F32)


def _dot_nt(a, b):
    return lax.dot_general(a, b, (((1,), (1,)), ((), ())), preferred_element_type=F32)


def _rms(x, g, n=None):
    n = x.shape[-1] if n is None else n
    ms = jnp.sum(x * x, axis=-1, keepdims=True) * (1.0 / n)
    return (x * lax.rsqrt(ms + RMS_EPS)) * g


def _sigmoid(x):
    return 1.0 / (1.0 + jnp.exp(-x))


def _params(sem, vmem=VMEM_LIMIT):
    return pltpu.CompilerParams(dimension_semantics=sem, vmem_limit_bytes=vmem)


def _in_proj_kernel(x_ref, g_ref, w_ref, o_ref):
    h = _rms(x_ref[...], g_ref[...]).astype(BF16)
    for n in range(N_PROJ // 256):
        sl = slice(n * 256, (n + 1) * 256)
        o_ref[:, sl] = _dot(h, w_ref[:, sl]).astype(BF16)


def _in_proj(x2, g, w):
    t = x2.shape[0]
    return pl.pallas_call(
        _in_proj_kernel,
        grid=(t // TM,),
        in_specs=[
            pl.BlockSpec((TM, D_MODEL), lambda i: (i, 0)),
            pl.BlockSpec((1, D_MODEL), lambda i: (0, 0)),
            pl.BlockSpec((D_MODEL, N_PROJ), lambda i: (0, 0)),
        ],
        out_specs=pl.BlockSpec((TM, N_PROJ), lambda i: (i, 0)),
        out_shape=jax.ShapeDtypeStruct((t, N_PROJ), BF16),
        compiler_params=_params(("parallel",)),
        name="in_proj",
    )(x2, g, w)


def _na_kernel(q_ref, k_ref, v_ref, b_ref, o_ref):
    m = pl.program_id(1)
    rows = k_ref.shape[0] // GRID_W
    kr0 = jnp.clip(NA_ROWS * m - NA_ROWS, 0, rows - NA_KROWS)
    start = pl.multiple_of(kr0 * GRID_W, NA_ROWS * GRID_W)
    nk = NA_KROWS * GRID_W
    lane = lax.broadcasted_iota(jnp.int32, (q_ref.shape[0], LANES), 1)
    scale = NA_HEAD_DIM ** -0.5

    def scores(h):
        pair = slice((h // 2) * LANES, (h // 2 + 1) * LANES)
        q = q_ref[:, pair]
        lo = (h % 2) * NA_HEAD_DIM
        qh = jnp.where((lane >= lo) & (lane < lo + NA_HEAD_DIM), q, jnp.zeros_like(q)) * scale
        return _dot_nt(qh, k_ref[pl.ds(start, nk), pair]) + b_ref[0, h]

    ones = jnp.ones((nk, LANES), BF16)
    outs = []
    s_next = scores(0)
    for h in range(NA_HEADS):
        s = s_next
        if h + 1 < NA_HEADS:
            s_next = scores(h + 1)
        pair = slice((h // 2) * LANES, (h // 2 + 1) * LANES)
        p = jnp.exp((s - jnp.max(s, axis=1, keepdims=True)).astype(BF16))
        ov = _dot(p, jnp.concatenate([v_ref[pl.ds(start, nk), pair], ones], axis=1))
        outs.append(ov[:, :LANES] / ov[:, LANES:])
        if h % 2 == 1:
            o_ref[:, pair] = jnp.where(lane < NA_HEAD_DIM, outs[h - 1], outs[h]).astype(BF16)


def _na_attention(proj, bias, batch, seq):
    rows = seq // GRID_W
    nblk = rows // NA_ROWS
    tq = NA_ROWS * GRID_W

    def variant(m):
        return jnp.where(m == 0, 0, jnp.where(m == nblk - 1, 2, 1))

    return pl.pallas_call(
        _na_kernel,
        grid=(batch, nblk),
        in_specs=[
            pl.BlockSpec((tq, NA_WIDTH), lambda b, m: (b * nblk + m, C_NA_Q // NA_WIDTH)),
            pl.BlockSpec((seq, NA_WIDTH), lambda b, m: (b, C_NA_K // NA_WIDTH)),
            pl.BlockSpec((seq, NA_WIDTH), lambda b, m: (b, C_NA_V // NA_WIDTH)),
            pl.BlockSpec((1, NA_HEADS, tq, NA_KROWS * GRID_W), lambda b, m: (variant(m), 0, 0, 0)),
        ],
        out_specs=pl.BlockSpec((tq, NA_WIDTH), lambda b, m: (b * nblk + m, 0)),
        out_shape=jax.ShapeDtypeStruct((batch * seq, NA_WIDTH), BF16),
        compiler_params=_params(("parallel", "arbitrary")),
        name="na_attention",
    )(proj, proj, proj, bias)


def _na_bias_tables(rpb, rows):
    nblk = rows // NA_ROWS
    qr = np.arange(NA_ROWS)
    kk = np.arange(NA_KROWS)
    row_sel = np.zeros((3, NA_ROWS, NA_KROWS, 2 * NA_WIN_ROWS - 1), np.float32)
    row_ok = np.zeros((3, NA_ROWS, NA_KROWS), bool)
    for vi, m in enumerate((0, 1, nblk - 1)):
        r = NA_ROWS * m + qr
        r0 = np.clip(r - NA_WIN_ROWS // 2, 0, rows - NA_WIN_ROWS)
        kr0 = np.clip(NA_ROWS * m - NA_ROWS, 0, rows - NA_KROWS)
        key_row = kr0 + kk
        ok = (key_row[None, :] >= r0[:, None]) & (key_row[None, :] < r0[:, None] + NA_WIN_ROWS)
        a = np.clip(key_row[None, :] - r[:, None] + NA_WIN_ROWS - 1, 0, 2 * NA_WIN_ROWS - 2)
        row_ok[vi] = ok
        row_sel[vi, qr[:, None], kk[None, :], a] = 1.0
    c = np.arange(GRID_W)
    cs = np.clip(c - NA_WIN_COLS // 2, 0, GRID_W - NA_WIN_COLS)
    col_ok = (c[None, :] >= cs[:, None]) & (c[None, :] < cs[:, None] + NA_WIN_COLS)
    bc = np.clip(c[None, :] - c[:, None] + NA_WIN_COLS - 1, 0, 2 * NA_WIN_COLS - 2)
    col_sel = np.zeros((GRID_W, GRID_W, 2 * NA_WIN_COLS - 1), np.float32)
    col_sel[c[:, None], c[None, :], bc] = 1.0
    bias = jnp.einsum("vqka,hab,cjb->vhqckj", jnp.asarray(row_sel), rpb.astype(F32), jnp.asarray(col_sel),
                      precision=lax.Precision.HIGHEST)
    ok = row_ok[:, None, :, None, :, None] & col_ok[None, None, None, :, None, :]
    bias = jnp.where(jnp.asarray(ok), bias, NEG_BIG)
    return bias.reshape(3, NA_HEADS, NA_ROWS * GRID_W, NA_KROWS * GRID_W)


def _mla_prep_kernel(ql_ref, kvl_ref, kr_ref, qn_ref, kvn_ref, wq_ref, wqr_ref, wk_ref, wv_ref,
                     ones_ref, cos_ref, sin_ref, q_out, k_out, v_out):
    cos = cos_ref[...]
    sin = sin_ref[...]
    cos8 = jnp.concatenate([cos] * MLA_HEADS, axis=1)
    sin8 = jnp.concatenate([sin] * MLA_HEADS, axis=1)
    hq = _rms(ql_ref[...].astype(F32), qn_ref[...], MLA_Q_RANK).astype(BF16)
    q = _dot(hq, wq_ref[...]) * cos8 + _dot(hq, wqr_ref[...]) * sin8
    q_out[...] = (q * MLA_QSCALE).astype(BF16)
    hkv = _rms(kvl_ref[...].astype(F32), kvn_ref[...]).astype(BF16)
    kr = kr_ref[...].astype(F32)
    kpe = kr[:, :HEAD_SLOT] * cos + kr[:, HEAD_SLOT:] * sin
    k = _dot(hkv, wk_ref[...]) + jnp.concatenate([kpe] * MLA_HEADS, axis=1)
    k_out[0] = k.T.astype(BF16)
    v_out[...] = (_dot(hkv, wv_ref[...]) + ones_ref[...]).astype(BF16)


def _mla_prep(proj, qn, kvn, wq, wqr, wk, wv, ones, cos_t, sin_t, seq):
    t = proj.shape[0]
    width = MLA_HEADS * HEAD_SLOT
    sblk = seq // TM
    full = lambda shape: pl.BlockSpec(shape, lambda i: (0, 0))
    out = jax.ShapeDtypeStruct((t, width), BF16)
    return pl.pallas_call(
        _mla_prep_kernel,
        grid=(t // TM,),
        in_specs=[
            pl.BlockSpec((TM, Q_LAT_PAD), lambda i: (i, C_Q_LAT // Q_LAT_PAD)),
            pl.BlockSpec((TM, MLA_KV_RANK), lambda i: (i, C_KV_LAT // MLA_KV_RANK)),
            pl.BlockSpec((TM, 2 * HEAD_SLOT), lambda i: (i, C_K_ROPE // (2 * HEAD_SLOT))),
            full((1, Q_LAT_PAD)), full((1, MLA_KV_RANK)),
            full((Q_LAT_PAD, width)), full((Q_LAT_PAD, width)),
            full((MLA_KV_RANK, width)), full((MLA_KV_RANK, width)),
            full((1, width)),
            pl.BlockSpec((TM, HEAD_SLOT), lambda i: (i % sblk, 0)),
            pl.BlockSpec((TM, HEAD_SLOT), lambda i: (i % sblk, 0)),
        ],
        out_specs=[pl.BlockSpec((TM, width), lambda i: (i, 0)),
                   pl.BlockSpec((1, width, TM), lambda i: (i // sblk, 0, i % sblk)),
                   pl.BlockSpec((TM, width), lambda i: (i, 0))],
        out_shape=[out, jax.ShapeDtypeStruct((t // seq, width, seq), BF16), out],
        compiler_params=_params(("parallel",)),
        name="mla_prep",
    )(proj, proj, proj, qn, kvn, wq, wqr, wk, wv, ones, cos_t, sin_t)


def _flash_kernel(q_ref, kt_ref, v_ref, o_ref):
    seq = v_ref.shape[0]
    nsub = q_ref.shape[0] // FLASH_TQ
    hslices = [slice(h * HEAD_SLOT, (h + 1) * HEAD_SLOT) for h in range(2)]
    streams = [(qi, h) for qi in range(nsub) for h in range(2)]
    qs = {(qi, h): q_ref[qi * FLASH_TQ:(qi + 1) * FLASH_TQ, hslices[h]] for qi, h in streams}
    ms = {st: jnp.full((FLASH_TQ, 1), NEG_BIG, F32) for st in streams}
    accs = {st: jnp.zeros((FLASH_TQ, HEAD_SLOT), F32) for st in streams}
    items = [(j, st) for j in range(seq // FLASH_TK) for st in streams]

    def scores(item):
        j, st = item
        return _dot(qs[st], kt_ref[0, hslices[st[1]], j * FLASH_TK:(j + 1) * FLASH_TK])

    s_next = scores(items[0])
    for n, (j, st) in enumerate(items):
        s = s_next
        if n + 1 < len(items):
            s_next = scores(items[n + 1])
        v = v_ref[j * FLASH_TK:(j + 1) * FLASH_TK, hslices[st[1]]]
        m_new = jnp.maximum(ms[st], jnp.max(s, axis=1, keepdims=True))
        p = jnp.exp2((s - m_new).astype(BF16))
        alpha = jnp.exp2(ms[st] - m_new)
        accs[st] = alpha * accs[st] + _dot(p, v)
        ms[st] = m_new
    lane = lax.broadcasted_iota(jnp.int32, (FLASH_TQ, HEAD_SLOT), 1)
    for qi in range(nsub):
        o0, o1 = [accs[(qi, h)] / pltpu.roll(accs[(qi, h)], MLA_V, axis=1) for h in range(2)]
        o_ref[qi * FLASH_TQ:(qi + 1) * FLASH_TQ, :] = jnp.where(lane < MLA_V, o0, o1).astype(BF16)


def _flash(qp, kp, vp, batch, seq):
    tq = FLASH_TQ * FLASH_SUBTILES
    nq = seq // tq
    return pl.pallas_call(
        _flash_kernel,
        grid=(batch, MLA_HEADS // 2, nq),
        in_specs=[
            pl.BlockSpec((tq, 2 * HEAD_SLOT), lambda b, hp, i: (b * nq + i, hp)),
            pl.BlockSpec((1, 2 * HEAD_SLOT, seq), lambda b, hp, i: (b, hp, 0)),
            pl.BlockSpec((seq, 2 * HEAD_SLOT), lambda b, hp, i: (b, hp)),
        ],
        out_specs=pl.BlockSpec((tq, LANES), lambda b, hp, i: (b * nq + i, hp)),
        out_shape=jax.ShapeDtypeStruct((batch * seq, MLA_WIDTH), BF16),
        compiler_params=_params(("parallel", "parallel", "arbitrary")),
        name="mla_flash",
    )(qp, kp, vp)


def _to_slab(o_ref, x):
    for j in range(D_MODEL // LANES):
        o_ref[pl.ds(j, x.shape[0], stride=D_MODEL // LANES), :] = x[:, j * LANES:(j + 1) * LANES]


def _from_slab(x_ref, rows):
    n = D_MODEL // LANES
    return jnp.concatenate([x_ref[pl.ds(j, rows, stride=n), :] for j in range(n)], axis=1)


def _merge_kernel(x_ref, ya_ref, yb_ref, ga_ref, gb_ref, wa_ref, wb_ref, wo_ref, gm_ref, wr_ref, o_ref, aff_ref):
    ya = _dot(ya_ref[...], wa_ref[...])
    yb = _dot(yb_ref[...], wb_ref[...])
    merged = _sigmoid(ga_ref[...].astype(F32)) * ya + _sigmoid(gb_ref[...].astype(F32)) * yb
    x = x_ref[...] + _dot(merged.astype(BF16), wo_ref[...])
    _to_slab(o_ref, x)
    h = _rms(x, gm_ref[...])
    h_hi = h.astype(BF16)
    h_lo = (h - h_hi.astype(F32)).astype(BF16)
    r = _dot(h_hi, wr_ref[...])
    logits = (r[:, :LANES] + r[:, LANES:]) + _dot(h_lo, wr_ref[:, :LANES])
    lane = lax.broadcasted_iota(jnp.int32, logits.shape, 1)
    logits = jnp.where(lane < N_EXPERTS, logits, NEG_BIG)
    e = jnp.exp(logits - jnp.max(logits, axis=1, keepdims=True))
    aff = e / jnp.sum(e, axis=1, keepdims=True)
    aff_ref[0] = aff.T[:N_EXPERTS, :]


def _merge(x2, ya, yb, proj, wa, wb, wo, gm, wr, seq):
    t = x2.shape[0]
    sblk = seq // TM
    nslab = D_MODEL // LANES
    full = lambda shape: pl.BlockSpec(shape, lambda i: (0, 0))
    return pl.pallas_call(
        _merge_kernel,
        grid=(t // TM,),
        in_specs=[
            pl.BlockSpec((TM, D_MODEL), lambda i: (i, 0)),
            pl.BlockSpec((TM, NA_WIDTH), lambda i: (i, 0)),
            pl.BlockSpec((TM, MLA_WIDTH), lambda i: (i, 0)),
            pl.BlockSpec((TM, D_MODEL), lambda i: (i, C_GATE_A // D_MODEL)),
            pl.BlockSpec((TM, D_MODEL), lambda i: (i, C_GATE_B // D_MODEL)),
            full((NA_WIDTH, D_MODEL)), full((MLA_WIDTH, D_MODEL)), full((D_MODEL, D_MODEL)),
            full((1, D_MODEL)), full((D_MODEL, 2 * LANES)),
        ],
        out_specs=[pl.BlockSpec((TM * nslab, LANES), lambda i: (i, 0)),
                   pl.BlockSpec((1, N_EXPERTS, TM), lambda i: (i // sblk, 0, i % sblk))],
        out_shape=[jax.ShapeDtypeStruct((t * nslab, LANES), F32),
                   jax.ShapeDtypeStruct((t // seq, N_EXPERTS, seq), F32)],
        compiler_params=_params(("parallel",)),
        name="merge",
    )(x2, ya, yb, proj, proj, wa, wb, wo, gm, wr)


def _prep_router(w):
    hi = w.astype(BF16)
    lo = (w - hi.astype(F32)).astype(BF16)
    z = jnp.zeros((D_MODEL, LANES - N_EXPERTS), BF16)
    return jnp.concatenate([hi, z, lo, z], axis=1)


def _cumsum_lanes(x):
    n = x.shape[1]
    lane = lax.broadcasted_iota(jnp.int32, x.shape, 1)
    sh = 1
    while sh < n:
        x = x + jnp.where(lane >= sh, pltpu.roll(x, sh, axis=1), 0.0)
        sh *= 2
    return x


def _select_kernel(aff_ref, val_ref, idx_ref, *, cap):
    aff = aff_ref[0]
    seq = aff.shape[1]
    bits = pltpu.bitcast(aff, jnp.int32)
    thr = jnp.zeros((N_EXPERTS, 1), jnp.int32)
    for bit in range(30, -1, -1):
        cand = thr | (1 << bit)
        cnt = jnp.sum((bits >= cand).astype(jnp.int32), axis=1, keepdims=True)
        thr = jnp.where(cnt >= cap, cand, thr)
    gt = bits > thr
    eq = bits == thr
    need = cap - jnp.sum(gt.astype(F32), axis=1, keepdims=True)
    sel = gt | (eq & (_cumsum_lanes(eq.astype(F32)) <= need))
    slot = jnp.where(sel, _cumsum_lanes(sel.astype(F32)) - 1.0, -1.0)

    hi = aff.astype(BF16).astype(F32)
    r1 = aff - hi
    mid = r1.astype(BF16).astype(F32)
    lo = (r1 - mid).astype(BF16).astype(F32)
    tok = lax.broadcasted_iota(jnp.int32, (1, seq), 1)
    t_hi = (tok >> 6).astype(F32)
    t_lo = (tok & 63).astype(F32)
    pad = jnp.zeros((SELECT_COLS - 5, seq), F32)
    cblk = 256
    for e in range(N_EXPERTS):
        rhs = jnp.concatenate([hi[e:e + 1], mid[e:e + 1], lo[e:e + 1], t_hi, t_lo, pad], axis=0).astype(BF16)
        slot_e = slot[e:e + 1, :]
        ci = lax.broadcasted_iota(jnp.int32, (cblk, 1), 0).astype(F32).astype(BF16)
        for cc in range(cap // cblk):
            rel = (slot_e - float(cc * cblk)).astype(BF16)
            hit = jnp.broadcast_to(rel, (cblk, seq)) == jnp.broadcast_to(ci, (cblk, seq))
            onehot = jnp.where(hit, jnp.ones((cblk, seq), BF16), jnp.zeros((cblk, seq), BF16))
            r = _dot_nt(onehot, rhs)
            cs = slice(cc * cblk, (cc + 1) * cblk)
            val_ref[0, e, cs, :] = (r[:, 0:1] + r[:, 1:2]) + r[:, 2:3]
            rt = r.T
            idx_ref[0, e, :, cs] = (rt[3:4, :] * 64.0 + rt[4:5, :]).astype(jnp.int32)


def _select(aff_t, cap):
    batch, _, seq = aff_t.shape
    return pl.pallas_call(
        functools.partial(_select_kernel, cap=cap),
        grid=(batch,),
        in_specs=[pl.BlockSpec((1, N_EXPERTS, seq), lambda b: (b, 0, 0))],
        out_specs=[pl.BlockSpec((1, N_EXPERTS, cap, 1), lambda b: (b, 0, 0, 0)),
                   pl.BlockSpec((1, N_EXPERTS, 1, cap), lambda b: (b, 0, 0, 0))],
        out_shape=[jax.ShapeDtypeStruct((batch, N_EXPERTS, cap, 1), F32),
                   jax.ShapeDtypeStruct((batch, N_EXPERTS, 1, cap), jnp.int32)],
        compiler_params=_params(("parallel",)),
        name="moe_select",
    )(aff_t)


def _gather_tokens(idx_ref, xs_ref, tile_ref, tokens):
    n = D_MODEL // LANES
    for c in tokens:
        row = pl.multiple_of(idx_ref[0, 0, c] * SUBLANES, SUBLANES)
        tile_ref[pl.ds(c, n, stride=TILE_PITCH), :] = xs_ref[pl.ds(row, SUBLANES), :]


def _scatter_tokens(idx_ref, tile_ref, acc_ref, tokens):
    n = D_MODEL // LANES
    rows, new = [], []
    for c in tokens:
        row = pl.multiple_of(idx_ref[0, 0, c] * SUBLANES, SUBLANES)
        rows.append(row)
        new.append(acc_ref[pl.ds(row, SUBLANES), :] + tile_ref[pl.ds(c, n, stride=TILE_PITCH), :])
    for row, val in zip(rows, new):
        acc_ref[pl.ds(row, SUBLANES), :] = val


def _tile_to_rows(tile_ref, cap):
    n = D_MODEL // LANES
    return jnp.concatenate([tile_ref[pl.ds(j * TILE_PITCH, cap), :] for j in range(n)], axis=1)


def _expert_ffn(xg_ref, w1_ref, w3_ref, w2_ref, side_work):
    part = None
    nsub = EXPERT_FF // MOE_SUB
    for k in range(nsub):
        cs = slice(k * MOE_SUB, (k + 1) * MOE_SUB)
        xg = xg_ref[...]
        h1 = _dot(xg, w1_ref[0, :, cs])
        h3 = _dot(xg, w3_ref[0, :, cs])
        act = ((h1 * _sigmoid(h1)) * h3).astype(BF16)
        d = _dot(act, w2_ref[0, cs, :])
        part = d if part is None else part + d
        side_work(k, nsub)
    return part


def _moe_kernel(idxp_ref, idxc_ref, idxn_ref, x_hbm, val_ref, g_ref, w1_ref, w3_ref, w2_ref, o_hbm,
                xs_ref, acc_ref, xtile_ref, ytile_ref, xg_ref, sem, *, cap):
    b = pl.program_id(0)
    e = pl.program_id(1)
    last_e = pl.num_programs(1) - 1
    nchunk = D_MODEL // LANES
    groups = cap // GATHER_UNROLL

    @pl.when(e == 0)
    def _start_batch_row():
        c0 = pltpu.make_async_copy(x_hbm.at[b], xs_ref, sem.at[0])
        c1 = pltpu.make_async_copy(x_hbm.at[b], acc_ref, sem.at[1])
        c0.start()
        c1.start()
        ytile_ref[...] = jnp.zeros_like(ytile_ref)
        c0.wait()
        c1.wait()

        def body(i, carry):
            _gather_tokens(idxc_ref, xs_ref, xtile_ref, [i * GATHER_UNROLL + u for u in range(GATHER_UNROLL)])
            return carry

        lax.fori_loop(0, groups, body, 0)
        xg_ref[0] = _rms(_tile_to_rows(xtile_ref, cap), g_ref[...]).astype(BF16)

    def side(k, nsub):
        half = nsub // 2
        if k < half:
            lo, hi = k * groups // half, (k + 1) * groups // half
            for grp in range(lo, hi):
                _scatter_tokens(idxp_ref, ytile_ref, acc_ref, range(grp * GATHER_UNROLL, (grp + 1) * GATHER_UNROLL))
        else:
            k -= half
            _gather_tokens(idxn_ref, xs_ref, xtile_ref, range(k * cap // half, (k + 1) * cap // half))

    y = _expert_ffn(xg_ref.at[e & 1], w1_ref, w3_ref, w2_ref, side) * val_ref[0]
    for j in range(nchunk):
        ytile_ref[pl.ds(j * TILE_PITCH, cap), :] = y[:, j * LANES:(j + 1) * LANES]
    xg_ref[(e + 1) & 1] = _rms(_tile_to_rows(xtile_ref, cap), g_ref[...]).astype(BF16)

    @pl.when(e == last_e)
    def _finish_batch_row():
        def body(i, carry):
            _scatter_tokens(idxc_ref, ytile_ref, acc_ref, [i * GATHER_UNROLL + u for u in range(GATHER_UNROLL)])
            return carry

        lax.fori_loop(0, groups, body, 0)
        c = pltpu.make_async_copy(acc_ref, o_hbm.at[b], sem.at[0])
        c.start()
        c.wait()


def _moe(x_slab, idx, val, g, w1, w3, w2, cap):
    batch, srows, _ = x_slab.shape
    assert (EXPERT_FF // MOE_SUB) % 2 == 0 and cap % GATHER_UNROLL == 0
    last = N_EXPERTS - 1
    idx_spec = lambda shift: pl.BlockSpec(
        (1, 1, cap), lambda b, e: (b * N_EXPERTS + jnp.clip(e + shift, 0, last), 0, 0), memory_space=pltpu.SMEM)
    tile = pltpu.VMEM((D_MODEL // LANES * TILE_PITCH, LANES), F32)
    return pl.pallas_call(
        functools.partial(_moe_kernel, cap=cap),
        grid=(batch, N_EXPERTS),
        in_specs=[
            idx_spec(-1), idx_spec(0), idx_spec(1),
            pl.BlockSpec(memory_space=pl.ANY),
            pl.BlockSpec((1, cap, 1), lambda b, e: (b * N_EXPERTS + e, 0, 0)),
            pl.BlockSpec((1, D_MODEL), lambda b, e: (0, 0)),
            pl.BlockSpec((1, D_MODEL, EXPERT_FF), lambda b, e: (e, 0, 0)),
            pl.BlockSpec((1, D_MODEL, EXPERT_FF), lambda b, e: (e, 0, 0)),
            pl.BlockSpec((1, EXPERT_FF, D_MODEL), lambda b, e: (e, 0, 0)),
        ],
        out_specs=pl.BlockSpec(memory_space=pl.ANY),
        out_shape=jax.ShapeDtypeStruct(x_slab.shape, F32),
        scratch_shapes=[
            pltpu.VMEM((srows, LANES), F32),
            pltpu.VMEM((srows, LANES), F32),
            tile, tile,
            pltpu.VMEM((2, cap, D_MODEL), BF16),
            pltpu.SemaphoreType.DMA((2,)),
        ],
        compiler_params=_params(("arbitrary", "arbitrary"), MOE_VMEM_LIMIT),
        name="moe_ffn",
    )(idx, idx, idx, x_slab, val, g, w1, w3, w2)


def _ple_kernel(x_ref, p_ref, g_ref, wg_ref, wp_ref, gf_ref, o_ref, *, final):
    x = _from_slab(x_ref, TM)
    gate = _sigmoid(_dot(_rms(x, g_ref[...]).astype(BF16), wg_ref[...]))
    y = x + gate * _dot(p_ref[0].astype(BF16), wp_ref[...])
    if final:
        y = _rms(y, gf_ref[...])
    o_ref[...] = y


def _ple(x_slab, p3, layer, g, wg, wp, gf, final):
    nslab = D_MODEL // LANES
    t = x_slab.shape[0] // nslab
    full = lambda shape: pl.BlockSpec(shape, lambda i: (0, 0))
    return pl.pallas_call(
        functools.partial(_ple_kernel, final=final),
        grid=(t // TM,),
        in_specs=[
            pl.BlockSpec((TM * nslab, LANES), lambda i: (i, 0)),
            pl.BlockSpec((1, TM, PLE_DIM), lambda i: (layer, i, 0)),
            full((1, D_MODEL)), full((D_MODEL, D_MODEL)), full((PLE_DIM, D_MODEL)), full((1, D_MODEL)),
        ],
        out_specs=pl.BlockSpec((TM, D_MODEL), lambda i: (i, 0)),
        out_shape=jax.ShapeDtypeStruct((t, D_MODEL), F32),
        compiler_params=_params(("parallel",)),
        name="ple",
    )(x_slab, p3, g, wg, wp, gf)


def _cast_kernel(w_ref, o_ref):
    o_ref[...] = w_ref[...].astype(BF16)


def _expert_weights_bf16(w, layer):
    _, e, r, c = w.shape
    return pl.pallas_call(
        _cast_kernel,
        grid=(e,),
        in_specs=[pl.BlockSpec((1, 1, r, c), lambda i: (layer, i, 0, 0))],
        out_specs=pl.BlockSpec((1, 1, r, c), lambda i: (0, i, 0, 0)),
        out_shape=jax.ShapeDtypeStruct((1, e, r, c), BF16),
        compiler_params=_params(("parallel",)),
        name="expert_weight_cast",
    )(w)[0]
def _rot_cols(w):
    q = MLA_ROPE // 4
    return jnp.concatenate([-w[:, q:2 * q], w[:, :q], -w[:, 3 * q:], w[:, 2 * q:3 * q]], axis=1)


def _prep_w_in(w):
    cuts = np.cumsum([NA_WIDTH, NA_WIDTH, NA_WIDTH, MLA_Q_RANK, MLA_KV_RANK, MLA_ROPE, D_MODEL])
    na_q, na_k, na_v, q_lat, kv_lat, k_rope, gate_a, gate_b = jnp.split(w, [int(c) for c in cuts], axis=1)
    z = lambda n: jnp.zeros((D_MODEL, n), w.dtype)
    out = jnp.concatenate([
        gate_a, gate_b, na_q, na_k, na_v,
        q_lat, z(Q_LAT_PAD - MLA_Q_RANK),
        kv_lat,
        z(MLA_NOPE), k_rope, z(HEAD_SLOT - MLA_NOPE - MLA_ROPE),
        z(MLA_NOPE), _rot_cols(k_rope), z(HEAD_SLOT - MLA_NOPE - MLA_ROPE),
    ], axis=1)
    assert out.shape[1] == N_PROJ
    return out.astype(BF16)


def _prep_mla_weights(wq_up, wkv_up):
    dqk = MLA_NOPE + MLA_ROPE
    wq = wq_up.reshape(MLA_Q_RANK, MLA_HEADS, dqk)
    nope, pe = wq[..., :MLA_NOPE], wq[..., MLA_NOPE:]
    pe_rot = _rot_cols(pe.reshape(MLA_Q_RANK * MLA_HEADS, MLA_ROPE)).reshape(MLA_Q_RANK, MLA_HEADS, MLA_ROPE)
    zq = lambda n: jnp.zeros((MLA_Q_RANK, MLA_HEADS, n), wq_up.dtype)
    tail = HEAD_SLOT - dqk
    wq_pad = jnp.concatenate([nope, pe, zq(tail)], axis=2).reshape(MLA_Q_RANK, MLA_HEADS * HEAD_SLOT)
    wq_rot = jnp.concatenate([zq(MLA_NOPE), pe_rot, zq(tail)], axis=2).reshape(MLA_Q_RANK, MLA_HEADS * HEAD_SLOT)
    rowpad = ((0, Q_LAT_PAD - MLA_Q_RANK), (0, 0))
    wq_pad = jnp.pad(wq_pad, rowpad).astype(BF16)
    wq_rot = jnp.pad(wq_rot, rowpad).astype(BF16)

    wkv = wkv_up.reshape(MLA_KV_RANK, MLA_HEADS, MLA_NOPE + MLA_V)
    k_nope, v = wkv[..., :MLA_NOPE], wkv[..., MLA_NOPE:]
    zk = jnp.zeros((MLA_KV_RANK, MLA_HEADS, HEAD_SLOT - MLA_NOPE), wkv_up.dtype)
    wk = jnp.concatenate([k_nope, zk], axis=2).reshape(MLA_KV_RANK, MLA_HEADS * HEAD_SLOT).astype(BF16)
    zv = jnp.zeros_like(v)
    even = (jnp.arange(MLA_HEADS) % 2 == 0)[None, :, None]
    wv = jnp.concatenate([jnp.where(even, v, zv), jnp.where(even, zv, v)], axis=2)
    wv = wv.reshape(MLA_KV_RANK, MLA_HEADS * HEAD_SLOT).astype(BF16)
    lane = np.arange(MLA_HEADS * HEAD_SLOT)
    head_even = (lane // HEAD_SLOT) % 2 == 0
    upper = (lane % HEAD_SLOT) >= MLA_V
    ones = jnp.asarray(np.where(head_even == upper, 1.0, 0.0).astype(np.float32))[None, :]
    return wq_pad, wq_rot, wk, wv, ones


def _rope_tables(seq):
    t = np.arange(seq)
    half = MLA_ROPE // 2
    freqs = 1.0 / (ROPE_BASE ** (jnp.arange(0, half, 2, dtype=F32) / half))

    def tab(pos):
        ang = jnp.asarray(pos, F32)[:, None] * freqs[None, :]
        return jnp.concatenate([jnp.cos(ang)] * 2, axis=1), jnp.concatenate([jnp.sin(ang)] * 2, axis=1)

    cr, sr = tab(t // GRID_W)
    cc, sc = tab(t % GRID_W)
    tail = HEAD_SLOT - MLA_NOPE - MLA_ROPE
    cos_t = jnp.concatenate([jnp.ones((seq, MLA_NOPE), F32), cr, cc, jnp.zeros((seq, tail), F32)], axis=1)
    sin_t = jnp.concatenate([jnp.zeros((seq, MLA_NOPE), F32), sr, sc, jnp.zeros((seq, tail), F32)], axis=1)
    return cos_t, sin_t


def kernel(x, p, norm_mix, w_in, na_rpb, mla_q_norm, mla_wq_up, mla_kv_norm, mla_wkv_up, w_na_o, w_mla_o, w_out,
           norm_moe, w_router, moe_w1, moe_w3, moe_w2, norm_ple, ple_gate_w, ple_w, norm_final):
    batch, seq, d = x.shape
    depth = w_in.shape[0]
    assert d == D_MODEL and seq % (GRID_W * NA_ROWS) == 0 and seq % TM == 0
    t = batch * seq
    cap = EC_CAPACITY * seq // N_EXPERTS
    assert TILE_PITCH == cap + SUBLANES and seq <= 64 * 64
    rows = seq // GRID_W
    cos_t, sin_t = _rope_tables(seq)
    row = lambda v: v.reshape(1, -1).astype(F32)

    x2 = x.reshape(t, d)
    for i in range(depth):
        proj = _in_proj(x2, row(norm_mix[i]), _prep_w_in(w_in[i]))
        ya = _na_attention(proj, _na_bias_tables(na_rpb[i], rows), batch, seq)
        wq, wqr, wk, wv, ones = _prep_mla_weights(mla_wq_up[i], mla_wkv_up[i])
        qn = jnp.pad(row(mla_q_norm[i]), ((0, 0), (0, Q_LAT_PAD - MLA_Q_RANK)))
        qp, kp, vp = _mla_prep(proj, qn, row(mla_kv_norm[i]), wq, wqr, wk, wv, ones, cos_t, sin_t, seq)
        yb = _flash(qp, kp, vp, batch, seq)
        x_slab, aff_t = _merge(x2, ya, yb, proj, w_na_o[i].astype(BF16), w_mla_o[i].astype(BF16),
                               w_out[i].astype(BF16), row(norm_moe[i]), _prep_router(w_router[i]), seq)

        val, idx = _select(aff_t, cap)
        x_slab = _moe(x_slab.reshape(batch, seq * (d // LANES), LANES),
                      idx.reshape(batch * N_EXPERTS, 1, cap), val.reshape(batch * N_EXPERTS, cap, 1),
                      row(norm_moe[i]), _expert_weights_bf16(moe_w1, i), _expert_weights_bf16(moe_w3, i),
                      _expert_weights_bf16(moe_w2, i), cap)

        x2 = _ple(x_slab.reshape(t * (d // LANES), LANES), p.reshape(depth, t, PLE_DIM), i, row(norm_ple[i]),
                  ple_gate_w[i].astype(BF16), ple_w[i].astype(BF16), row(norm_final), final=(i == depth - 1))
    return x2.reshape(batch, seq, d)
```

```python
import functools
import math

import jax
import jax.numpy as jnp
import numpy as np
from jax import lax
from jax.experimental import pallas as pl
from jax.experimental.pallas import tpu as pltpu

F32 = jnp.float32
BF16 = jnp.bfloat16

D_MODEL = 1024
GRID_W = 64
NA_HEADS = 8
NA_HEAD_DIM = 64
NA_WIN_ROWS = 8
NA_WIN_COLS = 16
NA_WIDTH = NA_HEADS * NA_HEAD_DIM
MLA_HEADS = 8
MLA_Q_RANK = 384
MLA_KV_RANK = 256
MLA_NOPE = 64
MLA_ROPE = 32
MLA_V = 64
MLA_WIDTH = MLA_HEADS * MLA_V
ROPE_BASE = 10000.0
N_EXPERTS = 16
EC_CAPACITY = 2
EXPERT_FF = 1024
PLE_DIM = 256
RMS_EPS = 1e-6

LANES = 128
SUBLANES = 8
HEAD_SLOT = 128
VMEM_LIMIT = 56 * 1024 * 1024

C_GATE_A = 0
C_GATE_B = 1024
C_NA_Q = 2048
C_NA_K = 2560
C_NA_V = 3072
C_Q_LAT = 3584
Q_LAT_PAD = 512
C_KV_LAT = 4096
C_K_ROPE = 4352
N_PROJ = 4608

TM = 512
NA_ROWS = 4
NA_KROWS = 12
FLASH_TQ = 256
FLASH_TK = 512
FLASH_SUBTILES = 4
MOE_SUB = 256
MOE_VMEM_LIMIT = 60 * 1024 * 1024
TILE_PITCH = 520
GATHER_UNROLL = 8
NEG_BIG = -1e30
SELECT_COLS = 16

MLA_QSCALE = float((MLA_NOPE + MLA_ROPE) ** -0.5 * math.log2(math.e))


def _dot(a, b):
    return jnp.dot(a, b, preferred_element_type=F32)


def _dot_nt(a, b):
    return lax.dot_general(a, b, (((1,), (1,)), ((), ())), preferred_element_type=F32)


def _rms(x, g, n=None):
    n = x.shape[-1] if n is None else n
    ms = jnp.sum(x * x, axis=-1, keepdims=True) * (1.0 / n)
    return (x * lax.rsqrt(ms + RMS_EPS)) * g


def _sigmoid(x):
    return 1.0 / (1.0 + jnp.exp(-x))


def _params(sem, vmem=VMEM_LIMIT):
    return pltpu.CompilerParams(dimension_semantics=sem, vmem_limit_bytes=vmem)


def _in_proj_kernel(x_ref, g_ref, w_ref, o_ref):
    h = _rms(x_ref[...], g_ref[...]).astype(BF16)
    for n in range(N_PROJ // 256):
        sl = slice(n * 256, (n + 1) * 256)
        o_ref[:, sl] = _dot(h, w_ref[:, sl]).astype(BF16)


def _in_proj(x2, g, w):
    t = x2.shape[0]
    return pl.pallas_call(
        _in_proj_kernel,
        grid=(t // TM,),
        in_specs=[
            pl.BlockSpec((TM, D_MODEL), lambda i: (i, 0)),
            pl.BlockSpec((1, D_MODEL), lambda i: (0, 0)),
            pl.BlockSpec((D_MODEL, N_PROJ), lambda i: (0, 0)),
        ],
        out_specs=pl.BlockSpec((TM, N_PROJ), lambda i: (i, 0)),
        out_shape=jax.ShapeDtypeStruct((t, N_PROJ), BF16),
        compiler_params=_params(("parallel",)),
        name="in_proj",
    )(x2, g, w)


def _na_kernel(q_ref, k_ref, v_ref, b_ref, o_ref):
    m = pl.program_id(1)
    rows = k_ref.shape[0] // GRID_W
    kr0 = jnp.clip(NA_ROWS * m - NA_ROWS, 0, rows - NA_KROWS)
    start = pl.multiple_of(kr0 * GRID_W, NA_ROWS * GRID_W)
    nk = NA_KROWS * GRID_W
    lane = lax.broadcasted_iota(jnp.int32, (q_ref.shape[0], LANES), 1)
    scale = NA_HEAD_DIM ** -0.5

    def scores(h):
        pair = slice((h // 2) * LANES, (h // 2 + 1) * LANES)
        q = q_ref[:, pair]
        lo = (h % 2) * NA_HEAD_DIM
        qh = jnp.where((lane >= lo) & (lane < lo + NA_HEAD_DIM), q, jnp.zeros_like(q)) * scale
        return _dot_nt(qh, k_ref[pl.ds(start, nk), pair]) + b_ref[0, h]

    ones = jnp.ones((nk, LANES), BF16)
    outs = []
    s_next = scores(0)
    for h in range(NA_HEADS):
        s = s_next
        if h + 1 < NA_HEADS:
            s_next = scores(h + 1)
        pair = slice((h // 2) * LANES, (h // 2 + 1) * LANES)
        p = jnp.exp((s - jnp.max(s, axis=1, keepdims=True)).astype(BF16))
        ov = _dot(p, jnp.concatenate([v_ref[pl.ds(start, nk), pair], ones], axis=1))
        outs.append(ov[:, :LANES] / ov[:, LANES:])
        if h % 2 == 1:
            o_ref[:, pair] = jnp.where(lane < NA_HEAD_DIM, outs[h - 1], outs[h]).astype(BF16)


def _na_attention(proj, bias, batch, seq):
    rows = seq // GRID_W
    nblk = rows // NA_ROWS
    tq = NA_ROWS * GRID_W

    def variant(m):
        return jnp.where(m == 0, 0, jnp.where(m == nblk - 1, 2, 1))

    return pl.pallas_call(
        _na_kernel,
        grid=(batch, nblk),
        in_specs=[
            pl.BlockSpec((tq, NA_WIDTH), lambda b, m: (b * nblk + m, C_NA_Q // NA_WIDTH)),
            pl.BlockSpec((seq, NA_WIDTH), lambda b, m: (b, C_NA_K // NA_WIDTH)),
            pl.BlockSpec((seq, NA_WIDTH), lambda b, m: (b, C_NA_V // NA_WIDTH)),
            pl.BlockSpec((1, NA_HEADS, tq, NA_KROWS * GRID_W), lambda b, m: (variant(m), 0, 0, 0)),
        ],
        out_specs=pl.BlockSpec((tq, NA_WIDTH), lambda b, m: (b * nblk + m, 0)),
        out_shape=jax.ShapeDtypeStruct((batch * seq, NA_WIDTH), BF16),
        compiler_params=_params(("parallel", "arbitrary")),
        name="na_attention",
    )(proj, proj, proj, bias)


def _na_bias_tables(rpb, rows):
    nblk = rows // NA_ROWS
    qr = np.arange(NA_ROWS)
    kk = np.arange(NA_KROWS)
    plan = []
    for m in (0, 1, nblk - 1):
        r = NA_ROWS * m + qr
        r0 = np.clip(r - NA_WIN_ROWS // 2, 0, rows - NA_WIN_ROWS)
        kr0 = np.clip(NA_ROWS * m - NA_ROWS, 0, rows - NA_KROWS)
        key_row = kr0 + kk
        ok = (key_row[None, :] >= r0[:, None]) & (key_row[None, :] < r0[:, None] + NA_WIN_ROWS)
        a = key_row[None, :] - r[:, None] + NA_WIN_ROWS - 1
        plan.append(np.where(ok, a, -1).tolist())
    c = np.arange(GRID_W)
    cs = np.clip(c - NA_WIN_COLS // 2, 0, GRID_W - NA_WIN_COLS)
    col_ok = (c[None, :] >= cs[:, None]) & (c[None, :] < cs[:, None] + NA_WIN_COLS)
    bc = np.clip(c[None, :] - c[:, None] + NA_WIN_COLS - 1, 0, 2 * NA_WIN_COLS - 2)
    col_sel = np.zeros((GRID_W, GRID_W, 2 * NA_WIN_COLS - 1), np.float32)
    col_sel[c[:, None], c[None, :], bc] = 1.0
    blocks = jnp.einsum("hab,cjb->hacj", rpb.astype(F32), jnp.asarray(col_sel), precision=lax.Precision.HIGHEST)
    blocks = jnp.where(jnp.asarray(col_ok)[None, None], blocks, NEG_BIG)
    nrow = 2 * NA_WIN_ROWS - 1
    return pl.pallas_call(
        functools.partial(_na_bias_kernel, plan=plan),
        grid=(NA_HEADS,),
        in_specs=[pl.BlockSpec((1, nrow, GRID_W, GRID_W), lambda h: (h, 0, 0, 0))],
        out_specs=pl.BlockSpec((3, 1, NA_ROWS * GRID_W, NA_KROWS * GRID_W), lambda h: (0, h, 0, 0)),
        out_shape=jax.ShapeDtypeStruct((3, NA_HEADS, NA_ROWS * GRID_W, NA_KROWS * GRID_W), F32),
        compiler_params=_params(("parallel",)),
        name="na_bias_table",
    )(blocks)


def _na_bias_kernel(t_ref, o_ref, *, plan):
    neg = jnp.full((GRID_W, GRID_W), NEG_BIG, F32)
    for v, rows_plan in enumerate(plan):
        for qr, keys in enumerate(rows_plan):
            for kk in range(0, NA_KROWS, 2):
                pair = [t_ref[0, a] if a >= 0 else neg for a in keys[kk:kk + 2]]
                o_ref[v, 0, qr * GRID_W:(qr + 1) * GRID_W, kk * GRID_W:(kk + 2) * GRID_W] = (
                    jnp.concatenate(pair, axis=1))


def _mla_prep_kernel(ql_ref, kvl_ref, kr_ref, qn_ref, kvn_ref, wq_ref, wqr_ref, wk_ref, wv_ref,
                     ones_ref, cos_ref, sin_ref, q_out, k_out, v_out):
    cos = cos_ref[...]
    sin = sin_ref[...]
    cos8 = jnp.concatenate([cos] * MLA_HEADS, axis=1)
    sin8 = jnp.concatenate([sin] * MLA_HEADS, axis=1)
    hq = _rms(ql_ref[...].astype(F32), qn_ref[...], MLA_Q_RANK).astype(BF16)
    q = _dot(hq, wq_ref[...]) * cos8 + _dot(hq, wqr_ref[...]) * sin8
    q_out[...] = (q * MLA_QSCALE).astype(BF16)
    hkv = _rms(kvl_ref[...].astype(F32), kvn_ref[...]).astype(BF16)
    kr = kr_ref[...].astype(F32)
    kpe = kr[:, :HEAD_SLOT] * cos + kr[:, HEAD_SLOT:] * sin
    k = _dot(hkv, wk_ref[...]) + jnp.concatenate([kpe] * MLA_HEADS, axis=1)
    k_out[0] = k.T.astype(BF16)
    v_out[...] = (_dot(hkv, wv_ref[...]) + ones_ref[...]).astype(BF16)


def _mla_prep(proj, qn, kvn, wq, wqr, wk, wv, ones, cos_t, sin_t, seq):
    t = proj.shape[0]
    width = MLA_HEADS * HEAD_SLOT
    sblk = seq // TM
    full = lambda shape: pl.BlockSpec(shape, lambda i: (0, 0))
    out = jax.ShapeDtypeStruct((t, width), BF16)
    return pl.pallas_call(
        _mla_prep_kernel,
        grid=(t // TM,),
        in_specs=[
            pl.BlockSpec((TM, Q_LAT_PAD), lambda i: (i, C_Q_LAT // Q_LAT_PAD)),
            pl.BlockSpec((TM, MLA_KV_RANK), lambda i: (i, C_KV_LAT // MLA_KV_RANK)),
            pl.BlockSpec((TM, 2 * HEAD_SLOT), lambda i: (i, C_K_ROPE // (2 * HEAD_SLOT))),
            full((1, Q_LAT_PAD)), full((1, MLA_KV_RANK)),
            full((Q_LAT_PAD, width)), full((Q_LAT_PAD, width)),
            full((MLA_KV_RANK, width)), full((MLA_KV_RANK, width)),
            full((1, width)),
            pl.BlockSpec((TM, HEAD_SLOT), lambda i: (i % sblk, 0)),
            pl.BlockSpec((TM, HEAD_SLOT), lambda i: (i % sblk, 0)),
        ],
        out_specs=[pl.BlockSpec((TM, width), lambda i: (i, 0)),
                   pl.BlockSpec((1, width, TM), lambda i: (i // sblk, 0, i % sblk)),
                   pl.BlockSpec((TM, width), lambda i: (i, 0))],
        out_shape=[out, jax.ShapeDtypeStruct((t // seq, width, seq), BF16), out],
        compiler_params=_params(("parallel",)),
        name="mla_prep",
    )(proj, proj, proj, qn, kvn, wq, wqr, wk, wv, ones, cos_t, sin_t)


def _flash_kernel(q_ref, kt_ref, v_ref, o_ref):
    seq = v_ref.shape[0]
    nsub = q_ref.shape[0] // FLASH_TQ
    hslices = [slice(h * HEAD_SLOT, (h + 1) * HEAD_SLOT) for h in range(2)]
    streams = [(qi, h) for qi in range(nsub) for h in range(2)]
    qs = {(qi, h): q_ref[qi * FLASH_TQ:(qi + 1) * FLASH_TQ, hslices[h]] for qi, h in streams}
    ms = {st: jnp.full((FLASH_TQ, 1), NEG_BIG, F32) for st in streams}
    accs = {st: jnp.zeros((FLASH_TQ, HEAD_SLOT), F32) for st in streams}
    items = [(j, st) for j in range(seq // FLASH_TK) for st in streams]

    def scores(item):
        j, st = item
        return _dot(qs[st], kt_ref[0, hslices[st[1]], j * FLASH_TK:(j + 1) * FLASH_TK])

    s_next = scores(items[0])
    for n, (j, st) in enumerate(items):
        s = s_next
        if n + 1 < len(items):
            s_next = scores(items[n + 1])
        v = v_ref[j * FLASH_TK:(j + 1) * FLASH_TK, hslices[st[1]]]
        m_new = jnp.maximum(ms[st], jnp.max(s, axis=1, keepdims=True))
        p = jnp.exp2((s - m_new).astype(BF16))
        alpha = jnp.exp2(ms[st] - m_new)
        accs[st] = alpha * accs[st] + _dot(p, v)
        ms[st] = m_new
    lane = lax.broadcasted_iota(jnp.int32, (FLASH_TQ, HEAD_SLOT), 1)
    for qi in range(nsub):
        o0, o1 = [accs[(qi, h)] / pltpu.roll(accs[(qi, h)], MLA_V, axis=1) for h in range(2)]
        o_ref[qi * FLASH_TQ:(qi + 1) * FLASH_TQ, :] = jnp.where(lane < MLA_V, o0, o1).astype(BF16)


def _flash(qp, kp, vp, batch, seq):
    tq = FLASH_TQ * FLASH_SUBTILES
    nq = seq // tq
    return pl.pallas_call(
        _flash_kernel,
        grid=(batch, MLA_HEADS // 2, nq),
        in_specs=[
            pl.BlockSpec((tq, 2 * HEAD_SLOT), lambda b, hp, i: (b * nq + i, hp)),
            pl.BlockSpec((1, 2 * HEAD_SLOT, seq), lambda b, hp, i: (b, hp, 0)),
            pl.BlockSpec((seq, 2 * HEAD_SLOT), lambda b, hp, i: (b, hp)),
        ],
        out_specs=pl.BlockSpec((tq, LANES), lambda b, hp, i: (b * nq + i, hp)),
        out_shape=jax.ShapeDtypeStruct((batch * seq, MLA_WIDTH), BF16),
        compiler_params=_params(("parallel", "parallel", "arbitrary")),
        name="mla_flash",
    )(qp, kp, vp)


def _to_slab(o_ref, x):
    for j in range(D_MODEL // LANES):
        o_ref[pl.ds(j, x.shape[0], stride=D_MODEL // LANES), :] = x[:, j * LANES:(j + 1) * LANES]


def _from_slab(x_ref, rows):
    n = D_MODEL // LANES
    return jnp.concatenate([x_ref[pl.ds(j, rows, stride=n), :] for j in range(n)], axis=1)


def _merge_kernel(x_ref, ya_ref, yb_ref, ga_ref, gb_ref, wa_ref, wb_ref, wo_ref, gm_ref, wr_ref, o_ref, aff_ref):
    ya = _dot(ya_ref[...], wa_ref[...])
    yb = _dot(yb_ref[...], wb_ref[...])
    merged = _sigmoid(ga_ref[...].astype(F32)) * ya + _sigmoid(gb_ref[...].astype(F32)) * yb
    x = x_ref[...] + _dot(merged.astype(BF16), wo_ref[...])
    _to_slab(o_ref, x)
    h = _rms(x, gm_ref[...])
    h_hi = h.astype(BF16)
    h_lo = (h - h_hi.astype(F32)).astype(BF16)
    r = _dot(h_hi, wr_ref[...])
    logits = (r[:, :LANES] + r[:, LANES:]) + _dot(h_lo, wr_ref[:, :LANES])
    lane = lax.broadcasted_iota(jnp.int32, logits.shape, 1)
    logits = jnp.where(lane < N_EXPERTS, logits, NEG_BIG)
    e = jnp.exp(logits - jnp.max(logits, axis=1, keepdims=True))
    aff = e / jnp.sum(e, axis=1, keepdims=True)
    aff_ref[0] = aff.T[:N_EXPERTS, :]


def _merge(x2, ya, yb, proj, wa, wb, wo, gm, wr, seq):
    t = x2.shape[0]
    sblk = seq // TM
    nslab = D_MODEL // LANES
    full = lambda shape: pl.BlockSpec(shape, lambda i: (0, 0))
    return pl.pallas_call(
        _merge_kernel,
        grid=(t // TM,),
        in_specs=[
            pl.BlockSpec((TM, D_MODEL), lambda i: (i, 0)),
            pl.BlockSpec((TM, NA_WIDTH), lambda i: (i, 0)),
            pl.BlockSpec((TM, MLA_WIDTH), lambda i: (i, 0)),
            pl.BlockSpec((TM, D_MODEL), lambda i: (i, C_GATE_A // D_MODEL)),
            pl.BlockSpec((TM, D_MODEL), lambda i: (i, C_GATE_B // D_MODEL)),
            full((NA_WIDTH, D_MODEL)), full((MLA_WIDTH, D_MODEL)), full((D_MODEL, D_MODEL)),
            full((1, D_MODEL)), full((D_MODEL, 2 * LANES)),
        ],
        out_specs=[pl.BlockSpec((TM * nslab, LANES), lambda i: (i, 0)),
                   pl.BlockSpec((1, N_EXPERTS, TM), lambda i: (i // sblk, 0, i % sblk))],
        out_shape=[jax.ShapeDtypeStruct((t * nslab, LANES), F32),
                   jax.ShapeDtypeStruct((t // seq, N_EXPERTS, seq), F32)],
        compiler_params=_params(("parallel",)),
        name="merge",
    )(x2, ya, yb, proj, proj, wa, wb, wo, gm, wr)


def _prep_router(w):
    hi = w.astype(BF16)
    lo = (w - hi.astype(F32)).astype(BF16)
    z = jnp.zeros((D_MODEL, LANES - N_EXPERTS), BF16)
    return jnp.concatenate([hi, z, lo, z], axis=1)


def _cumsum_lanes(x):
    n = x.shape[1]
    lane = lax.broadcasted_iota(jnp.int32, x.shape, 1)
    sh = 1
    while sh < n:
        x = x + jnp.where(lane >= sh, pltpu.roll(x, sh, axis=1), 0.0)
        sh *= 2
    return x


def _select_kernel(aff_ref, val_ref, idx_ref, *, cap):
    aff = aff_ref[0]
    seq = aff.shape[1]
    bits = pltpu.bitcast(aff, jnp.int32)
    thr = jnp.zeros((N_EXPERTS, 1), jnp.int32)
    for bit in range(30, -1, -1):
        cand = thr | (1 << bit)
        cnt = jnp.sum((bits >= cand).astype(jnp.int32), axis=1, keepdims=True)
        thr = jnp.where(cnt >= cap, cand, thr)
    gt = bits > thr
    eq = bits == thr
    need = cap - jnp.sum(gt.astype(F32), axis=1, keepdims=True)
    sel = gt | (eq & (_cumsum_lanes(eq.astype(F32)) <= need))
    slot = jnp.where(sel, _cumsum_lanes(sel.astype(F32)) - 1.0, -1.0)

    hi = aff.astype(BF16).astype(F32)
    r1 = aff - hi
    mid = r1.astype(BF16).astype(F32)
    lo = (r1 - mid).astype(BF16).astype(F32)
    tok = lax.broadcasted_iota(jnp.int32, (1, seq), 1)
    t_hi = (tok >> 6).astype(F32)
    t_lo = (tok & 63).astype(F32)
    pad = jnp.zeros((SELECT_COLS - 5, seq), F32)
    cblk = 256
    for e in range(N_EXPERTS):
        rhs = jnp.concatenate([hi[e:e + 1], mid[e:e + 1], lo[e:e + 1], t_hi, t_lo, pad], axis=0).astype(BF16)
        slot_e = slot[e:e + 1, :]
        ci = lax.broadcasted_iota(jnp.int32, (cblk, 1), 0).astype(F32).astype(BF16)
        for cc in range(cap // cblk):
            rel = (slot_e - float(cc * cblk)).astype(BF16)
            hit = jnp.broadcast_to(rel, (cblk, seq)) == jnp.broadcast_to(ci, (cblk, seq))
            onehot = jnp.where(hit, jnp.ones((cblk, seq), BF16), jnp.zeros((cblk, seq), BF16))
            r = _dot_nt(onehot, rhs)
            cs = slice(cc * cblk, (cc + 1) * cblk)
            val_ref[0, e, cs, :] = (r[:, 0:1] + r[:, 1:2]) + r[:, 2:3]
            rt = r.T
            idx_ref[0, e, :, cs] = (rt[3:4, :] * 64.0 + rt[4:5, :]).astype(jnp.int32)


def _select(aff_t, cap):
    batch, _, seq = aff_t.shape
    return pl.pallas_call(
        functools.partial(_select_kernel, cap=cap),
        grid=(batch,),
        in_specs=[pl.BlockSpec((1, N_EXPERTS, seq), lambda b: (b, 0, 0))],
        out_specs=[pl.BlockSpec((1, N_EXPERTS, cap, 1), lambda b: (b, 0, 0, 0)),
                   pl.BlockSpec((1, N_EXPERTS, 1, cap), lambda b: (b, 0, 0, 0))],
        out_shape=[jax.ShapeDtypeStruct((batch, N_EXPERTS, cap, 1), F32),
                   jax.ShapeDtypeStruct((batch, N_EXPERTS, 1, cap), jnp.int32)],
        compiler_params=_params(("parallel",)),
        name="moe_select",
    )(aff_t)


def _gather_tokens(idx_ref, xs_ref, tile_ref, tokens):
    n = D_MODEL // LANES
    for c in tokens:
        row = pl.multiple_of(idx_ref[0, 0, c] * SUBLANES, SUBLANES)
        tile_ref[pl.ds(c, n, stride=TILE_PITCH), :] = xs_ref[pl.ds(row, SUBLANES), :]


def _scatter_tokens(idx_ref, tile_ref, acc_ref, tokens):
    n = D_MODEL // LANES
    rows, new = [], []
    for c in tokens:
        row = pl.multiple_of(idx_ref[0, 0, c] * SUBLANES, SUBLANES)
        rows.append(row)
        new.append(acc_ref[pl.ds(row, SUBLANES), :] + tile_ref[pl.ds(c, n, stride=TILE_PITCH), :])
    for row, val in zip(rows, new):
        acc_ref[pl.ds(row, SUBLANES), :] = val


def _tile_to_rows(tile_ref, cap):
    n = D_MODEL // LANES
    return jnp.concatenate([tile_ref[pl.ds(j * TILE_PITCH, cap), :] for j in range(n)], axis=1)


def _expert_ffn(xg_ref, w1_ref, w3_ref, w2_ref, side_work, finish):
    nsub = EXPERT_FF // MOE_SUB
    acts = []
    for k in range(nsub):
        cs = slice(k * MOE_SUB, (k + 1) * MOE_SUB)
        xg = xg_ref[...]
        h1 = _dot(xg, w1_ref[0, :, cs])
        h3 = _dot(xg, w3_ref[0, :, cs])
        acts.append(((h1 * _sigmoid(h1)) * h3).astype(BF16))
        side_work(k, 2 * nsub)
    act = jnp.concatenate(acts, axis=1)
    outs = []
    nout = D_MODEL // MOE_SUB
    for n in range(nout):
        outs.append(finish(n, _dot(act, w2_ref[0, :, n * MOE_SUB:(n + 1) * MOE_SUB])))
        side_work(nsub + n, 2 * nsub)
    return outs


def _moe_kernel(idxp_ref, idxc_ref, idxn_ref, x_hbm, val_ref, g_ref, w1_ref, w3_ref, w2_ref, o_hbm,
                xs_ref, acc_ref, xtile_ref, ytile_ref, xg_ref, sem, *, cap):
    b = pl.program_id(0)
    e = pl.program_id(1)
    last_e = pl.num_programs(1) - 1
    nchunk = D_MODEL // LANES
    groups = cap // GATHER_UNROLL

    @pl.when(e == 0)
    def _start_batch_row():
        c0 = pltpu.make_async_copy(x_hbm.at[b], xs_ref, sem.at[0])
        c1 = pltpu.make_async_copy(x_hbm.at[b], acc_ref, sem.at[1])
        c0.start()
        c1.start()
        ytile_ref[...] = jnp.zeros_like(ytile_ref)
        c0.wait()
        c1.wait()

        def body(i, carry):
            _gather_tokens(idxc_ref, xs_ref, xtile_ref, [i * GATHER_UNROLL + u for u in range(GATHER_UNROLL)])
            return carry

        lax.fori_loop(0, groups, body, 0)
        xg_ref[0] = _rms(_tile_to_rows(xtile_ref, cap), g_ref[...]).astype(BF16)

    def side(k, nsub):
        half = nsub // 2
        if k < half:
            lo, hi = k * groups // half, (k + 1) * groups // half
            for grp in range(lo, hi):
                _scatter_tokens(idxp_ref, ytile_ref, acc_ref, range(grp * GATHER_UNROLL, (grp + 1) * GATHER_UNROLL))
        else:
            k -= half
            _gather_tokens(idxn_ref, xs_ref, xtile_ref, range(k * cap // half, (k + 1) * cap // half))

    def finish(n, block):
        y = block * val_ref[0]
        per = MOE_SUB // LANES
        for j in range(per):
            ytile_ref[pl.ds((n * per + j) * TILE_PITCH, cap), :] = y[:, j * LANES:(j + 1) * LANES]

    _expert_ffn(xg_ref.at[e & 1], w1_ref, w3_ref, w2_ref, side, finish)
    xg_ref[(e + 1) & 1] = _rms(_tile_to_rows(xtile_ref, cap), g_ref[...]).astype(BF16)

    @pl.when(e == last_e)
    def _finish_batch_row():
        def body(i, carry):
            _scatter_tokens(idxc_ref, ytile_ref, acc_ref, [i * GATHER_UNROLL + u for u in range(GATHER_UNROLL)])
            return carry

        lax.fori_loop(0, groups, body, 0)
        c = pltpu.make_async_copy(acc_ref, o_hbm.at[b], sem.at[0])
        c.start()
        c.wait()


def _moe(x_slab, idx, val, g, w1, w3, w2, cap):
    batch, srows, _ = x_slab.shape
    assert (EXPERT_FF // MOE_SUB) % 2 == 0 and cap % GATHER_UNROLL == 0
    last = N_EXPERTS - 1
    idx_spec = lambda shift: pl.BlockSpec(
        (1, 1, cap), lambda b, e: (b * N_EXPERTS + jnp.clip(e + shift, 0, last), 0, 0), memory_space=pltpu.SMEM)
    tile = pltpu.VMEM((D_MODEL // LANES * TILE_PITCH, LANES), F32)
    return pl.pallas_call(
        functools.partial(_moe_kernel, cap=cap),
        grid=(batch, N_EXPERTS),
        in_specs=[
            idx_spec(-1), idx_spec(0), idx_spec(1),
            pl.BlockSpec(memory_space=pl.ANY),
            pl.BlockSpec((1, cap, 1), lambda b, e: (b * N_EXPERTS + e, 0, 0)),
            pl.BlockSpec((1, D_MODEL), lambda b, e: (0, 0)),
            pl.BlockSpec((1, D_MODEL, EXPERT_FF), lambda b, e: (e, 0, 0)),
            pl.BlockSpec((1, D_MODEL, EXPERT_FF), lambda b, e: (e, 0, 0)),
            pl.BlockSpec((1, EXPERT_FF, D_MODEL), lambda b, e: (e, 0, 0)),
        ],
        out_specs=pl.BlockSpec(memory_space=pl.ANY),
        out_shape=jax.ShapeDtypeStruct(x_slab.shape, F32),
        scratch_shapes=[
            pltpu.VMEM((srows, LANES), F32),
            pltpu.VMEM((srows, LANES), F32),
            tile, tile,
            pltpu.VMEM((2, cap, D_MODEL), BF16),
            pltpu.SemaphoreType.DMA((2,)),
        ],
        compiler_params=_params(("arbitrary", "arbitrary"), MOE_VMEM_LIMIT),
        name="moe_ffn",
    )(idx, idx, idx, x_slab, val, g, w1, w3, w2)


def _ple_kernel(x_ref, p_ref, g_ref, wg_ref, wp_ref, gf_ref, o_ref, *, final):
    x = _from_slab(x_ref, TM)
    gate = _sigmoid(_dot(_rms(x, g_ref[...]).astype(BF16), wg_ref[...]))
    y = x + gate * _dot(p_ref[0].astype(BF16), wp_ref[...])
    if final:
        y = _rms(y, gf_ref[...])
    o_ref[...] = y


def _ple(x_slab, p3, layer, g, wg, wp, gf, final):
    nslab = D_MODEL // LANES
    t = x_slab.shape[0] // nslab
    full = lambda shape: pl.BlockSpec(shape, lambda i: (0, 0))
    return pl.pallas_call(
        functools.partial(_ple_kernel, final=final),
        grid=(t // TM,),
        in_specs=[
            pl.BlockSpec((TM * nslab, LANES), lambda i: (i, 0)),
            pl.BlockSpec((1, TM, PLE_DIM), lambda i: (layer, i, 0)),
            full((1, D_MODEL)), full((D_MODEL, D_MODEL)), full((PLE_DIM, D_MODEL)), full((1, D_MODEL)),
        ],
        out_specs=pl.BlockSpec((TM, D_MODEL), lambda i: (i, 0)),
        out_shape=jax.ShapeDtypeStruct((t, D_MODEL), F32),
        compiler_params=_params(("parallel",)),
        name="ple",
    )(x_slab, p3, g, wg, wp, gf)


def _cast_kernel(w_ref, o_ref):
    o_ref[...] = w_ref[...].astype(BF16)


def _expert_weights_bf16(w, layer):
    _, e, r, c = w.shape
    return pl.pallas_call(
        _cast_kernel,
        grid=(e,),
        in_specs=[pl.BlockSpec((1, 1, r, c), lambda i: (layer, i, 0, 0))],
        out_specs=pl.BlockSpec((1, 1, r, c), lambda i: (0, i, 0, 0)),
        out_shape=jax.ShapeDtypeStruct((1, e, r, c), BF16),
        compiler_params=_params(("parallel",)),
        name="expert_weight_cast",
    )(w)[0]
def _rot_cols(w):
    q = MLA_ROPE // 4
    return jnp.concatenate([-w[:, q:2 * q], w[:, :q], -w[:, 3 * q:], w[:, 2 * q:3 * q]], axis=1)


def _prep_w_in(w):
    cuts = np.cumsum([NA_WIDTH, NA_WIDTH, NA_WIDTH, MLA_Q_RANK, MLA_KV_RANK, MLA_ROPE, D_MODEL])
    na_q, na_k, na_v, q_lat, kv_lat, k_rope, gate_a, gate_b = jnp.split(w, [int(c) for c in cuts], axis=1)
    z = lambda n: jnp.zeros((D_MODEL, n), w.dtype)
    out = jnp.concatenate([
        gate_a, gate_b, na_q, na_k, na_v,
        q_lat, z(Q_LAT_PAD - MLA_Q_RANK),
        kv_lat,
        z(MLA_NOPE), k_rope, z(HEAD_SLOT - MLA_NOPE - MLA_ROPE),
        z(MLA_NOPE), _rot_cols(k_rope), z(HEAD_SLOT - MLA_NOPE - MLA_ROPE),
    ], axis=1)
    assert out.shape[1] == N_PROJ
    return out.astype(BF16)


def _prep_mla_weights(wq_up, wkv_up):
    dqk = MLA_NOPE + MLA_ROPE
    wq = wq_up.reshape(MLA_Q_RANK, MLA_HEADS, dqk)
    nope, pe = wq[..., :MLA_NOPE], wq[..., MLA_NOPE:]
    pe_rot = _rot_cols(pe.reshape(MLA_Q_RANK * MLA_HEADS, MLA_ROPE)).reshape(MLA_Q_RANK, MLA_HEADS, MLA_ROPE)
    zq = lambda n: jnp.zeros((MLA_Q_RANK, MLA_HEADS, n), wq_up.dtype)
    tail = HEAD_SLOT - dqk
    wq_pad = jnp.concatenate([nope, pe, zq(tail)], axis=2).reshape(MLA_Q_RANK, MLA_HEADS * HEAD_SLOT)
    wq_rot = jnp.concatenate([zq(MLA_NOPE), pe_rot, zq(tail)], axis=2).reshape(MLA_Q_RANK, MLA_HEADS * HEAD_SLOT)
    rowpad = ((0, Q_LAT_PAD - MLA_Q_RANK), (0, 0))
    wq_pad = jnp.pad(wq_pad, rowpad).astype(BF16)
    wq_rot = jnp.pad(wq_rot, rowpad).astype(BF16)

    wkv = wkv_up.reshape(MLA_KV_RANK, MLA_HEADS, MLA_NOPE + MLA_V)
    k_nope, v = wkv[..., :MLA_NOPE], wkv[..., MLA_NOPE:]
    zk = jnp.zeros((MLA_KV_RANK, MLA_HEADS, HEAD_SLOT - MLA_NOPE), wkv_up.dtype)
    wk = jnp.concatenate([k_nope, zk], axis=2).reshape(MLA_KV_RANK, MLA_HEADS * HEAD_SLOT).astype(BF16)
    zv = jnp.zeros_like(v)
    even = (jnp.arange(MLA_HEADS) % 2 == 0)[None, :, None]
    wv = jnp.concatenate([jnp.where(even, v, zv), jnp.where(even, zv, v)], axis=2)
    wv = wv.reshape(MLA_KV_RANK, MLA_HEADS * HEAD_SLOT).astype(BF16)
    lane = np.arange(MLA_HEADS * HEAD_SLOT)
    head_even = (lane // HEAD_SLOT) % 2 == 0
    upper = (lane % HEAD_SLOT) >= MLA_V
    ones = jnp.asarray(np.where(head_even == upper, 1.0, 0.0).astype(np.float32))[None, :]
    return wq_pad, wq_rot, wk, wv, ones


def _rope_tables(seq):
    t = np.arange(seq)
    half = MLA_ROPE // 2
    freqs = 1.0 / (ROPE_BASE ** (jnp.arange(0, half, 2, dtype=F32) / half))

    def tab(pos):
        ang = jnp.asarray(pos, F32)[:, None] * freqs[None, :]
        return jnp.concatenate([jnp.cos(ang)] * 2, axis=1), jnp.concatenate([jnp.sin(ang)] * 2, axis=1)

    cr, sr = tab(t // GRID_W)
    cc, sc = tab(t % GRID_W)
    tail = HEAD_SLOT - MLA_NOPE - MLA_ROPE
    cos_t = jnp.concatenate([jnp.ones((seq, MLA_NOPE), F32), cr, cc, jnp.zeros((seq, tail), F32)], axis=1)
    sin_t = jnp.concatenate([jnp.zeros((seq, MLA_NOPE), F32), sr, sc, jnp.zeros((seq, tail), F32)], axis=1)
    return cos_t, sin_t


def kernel(x, p, norm_mix, w_in, na_rpb, mla_q_norm, mla_wq_up, mla_kv_norm, mla_wkv_up, w_na_o, w_mla_o, w_out,
           norm_moe, w_router, moe_w1, moe_w3, moe_w2, norm_ple, ple_gate_w, ple_w, norm_final):
    batch, seq, d = x.shape
    depth = w_in.shape[0]
    assert d == D_MODEL and seq % (GRID_W * NA_ROWS) == 0 and seq % TM == 0
    t = batch * seq
    cap = EC_CAPACITY * seq // N_EXPERTS
    assert TILE_PITCH == cap + SUBLANES and seq <= 64 * 64
    rows = seq // GRID_W
    cos_t, sin_t = _rope_tables(seq)
    row = lambda v: v.reshape(1, -1).astype(F32)

    x2 = x.reshape(t, d)
    for i in range(depth):
        proj = _in_proj(x2, row(norm_mix[i]), _prep_w_in(w_in[i]))
        ya = _na_attention(proj, _na_bias_tables(na_rpb[i], rows), batch, seq)
        wq, wqr, wk, wv, ones = _prep_mla_weights(mla_wq_up[i], mla_wkv_up[i])
        qn = jnp.pad(row(mla_q_norm[i]), ((0, 0), (0, Q_LAT_PAD - MLA_Q_RANK)))
        qp, kp, vp = _mla_prep(proj, qn, row(mla_kv_norm[i]), wq, wqr, wk, wv, ones, cos_t, sin_t, seq)
        yb = _flash(qp, kp, vp, batch, seq)
        x_slab, aff_t = _merge(x2, ya, yb, proj, w_na_o[i].astype(BF16), w_mla_o[i].astype(BF16),
                               w_out[i].astype(BF16), row(norm_moe[i]), _prep_router(w_router[i]), seq)

        val, idx = _select(aff_t, cap)
        x_slab = _moe(x_slab.reshape(batch, seq * (d // LANES), LANES),
                      idx.reshape(batch * N_EXPERTS, 1, cap), val.reshape(batch * N_EXPERTS, cap, 1),
                      row(norm_moe[i]), _expert_weights_bf16(moe_w1, i), _expert_weights_bf16(moe_w3, i),
                      _expert_weights_bf16(moe_w2, i), cap)

        x2 = _ple(x_slab.reshape(t * (d // LANES), LANES), p.reshape(depth, t, PLE_DIM), i, row(norm_ple[i]),
                  ple_gate_w[i].astype(BF16), ple_w[i].astype(BF16), row(norm_final), final=(i == depth - 1))
    return x2.reshape(batch, seq, d)
```

```python
import functools
import math

import jax
import jax.numpy as jnp
import numpy as np
from jax import lax
from jax.experimental import pallas as pl
from jax.experimental.pallas import tpu as pltpu

F32 = jnp.float32
BF16 = jnp.bfloat16

D_MODEL = 1024
GRID_W = 64
NA_HEADS = 8
NA_HEAD_DIM = 64
NA_WIN_ROWS = 8
NA_WIN_COLS = 16
NA_WIDTH = NA_HEADS * NA_HEAD_DIM
MLA_HEADS = 8
MLA_Q_RANK = 384
MLA_KV_RANK = 256
MLA_NOPE = 64
MLA_ROPE = 32
MLA_V = 64
MLA_WIDTH = MLA_HEADS * MLA_V
ROPE_BASE = 10000.0
N_EXPERTS = 16
EC_CAPACITY = 2
EXPERT_FF = 1024
PLE_DIM = 256
RMS_EPS = 1e-6

LANES = 128
SUBLANES = 8
HEAD_SLOT = 128
VMEM_LIMIT = 56 * 1024 * 1024

C_GATE_A = 0
C_GATE_B = 1024
C_NA_Q = 2048
C_NA_K = 2560
C_NA_V = 3072
C_Q_LAT = 3584
Q_LAT_PAD = 512
C_KV_LAT = 4096
C_K_ROPE = 4352
N_PROJ = 4608

TM = 1024
NA_ROWS = 4
NA_KROWS = 12
FLASH_TQ = 256
FLASH_TK = 512
FLASH_SUBTILES = 4
MOE_SUB = 256
MOE_VMEM_LIMIT = 60 * 1024 * 1024
TILE_PITCH = 520
GATHER_UNROLL = 8
NEG_BIG = -1e30
SELECT_COLS = 16

MLA_QSCALE = float((MLA_NOPE + MLA_ROPE) ** -0.5 * math.log2(math.e))


def _dot(a, b):
    return jnp.dot(a, b, preferred_element_type=F32)


def _dot_nt(a, b):
    return lax.dot_general(a, b, (((1,), (1,)), ((), ())), preferred_element_type=F32)


def _rms(x, g, n=None):
    n = x.shape[-1] if n is None else n
    ms = jnp.sum(x * x, axis=-1, keepdims=True) * (1.0 / n)
    return (x * lax.rsqrt(ms + RMS_EPS)) * g


def _sigmoid(x):
    return 1.0 / (1.0 + jnp.exp(-x))


def _params(sem, vmem=VMEM_LIMIT):
    return pltpu.CompilerParams(dimension_semantics=sem, vmem_limit_bytes=vmem)


def _in_proj_kernel(x_ref, g_ref, w_ref, o_ref):
    h = _rms(x_ref[...], g_ref[...]).astype(BF16)
    for n in range(N_PROJ // 256):
        sl = slice(n * 256, (n + 1) * 256)
        o_ref[:, sl] = _dot(h, w_ref[:, sl]).astype(BF16)


def _in_proj(x2, g, w):
    t = x2.shape[0]
    return pl.pallas_call(
        _in_proj_kernel,
        grid=(t // TM,),
        in_specs=[
            pl.BlockSpec((TM, D_MODEL), lambda i: (i, 0)),
            pl.BlockSpec((1, D_MODEL), lambda i: (0, 0)),
            pl.BlockSpec((D_MODEL, N_PROJ), lambda i: (0, 0)),
        ],
        out_specs=pl.BlockSpec((TM, N_PROJ), lambda i: (i, 0)),
        out_shape=jax.ShapeDtypeStruct((t, N_PROJ), BF16),
        compiler_params=_params(("parallel",)),
        name="in_proj",
    )(x2, g, w)


def _na_kernel(q_ref, k_ref, v_ref, b_ref, o_ref):
    m = pl.program_id(1)
    rows = k_ref.shape[0] // GRID_W
    kr0 = jnp.clip(NA_ROWS * m - NA_ROWS, 0, rows - NA_KROWS)
    start = pl.multiple_of(kr0 * GRID_W, NA_ROWS * GRID_W)
    nk = NA_KROWS * GRID_W
    lane = lax.broadcasted_iota(jnp.int32, (q_ref.shape[0], LANES), 1)
    scale = NA_HEAD_DIM ** -0.5

    def scores(h):
        pair = slice((h // 2) * LANES, (h // 2 + 1) * LANES)
        q = q_ref[:, pair]
        lo = (h % 2) * NA_HEAD_DIM
        qh = jnp.where((lane >= lo) & (lane < lo + NA_HEAD_DIM), q, jnp.zeros_like(q)) * scale
        return _dot_nt(qh, k_ref[pl.ds(start, nk), pair]) + b_ref[0, h]

    ones = jnp.ones((nk, LANES), BF16)
    outs = []
    s_next = scores(0)
    for h in range(NA_HEADS):
        s = s_next
        if h + 1 < NA_HEADS:
            s_next = scores(h + 1)
        pair = slice((h // 2) * LANES, (h // 2 + 1) * LANES)
        p = jnp.exp((s - jnp.max(s, axis=1, keepdims=True)).astype(BF16))
        ov = _dot(p, jnp.concatenate([v_ref[pl.ds(start, nk), pair], ones], axis=1))
        outs.append(ov[:, :LANES] / ov[:, LANES:])
        if h % 2 == 1:
            o_ref[:, pair] = jnp.where(lane < NA_HEAD_DIM, outs[h - 1], outs[h]).astype(BF16)


def _na_attention(proj, bias, batch, seq):
    rows = seq // GRID_W
    nblk = rows // NA_ROWS
    tq = NA_ROWS * GRID_W

    def variant(m):
        return jnp.where(m == 0, 0, jnp.where(m == nblk - 1, 2, 1))

    return pl.pallas_call(
        _na_kernel,
        grid=(batch, nblk),
        in_specs=[
            pl.BlockSpec((tq, NA_WIDTH), lambda b, m: (b * nblk + m, C_NA_Q // NA_WIDTH)),
            pl.BlockSpec((seq, NA_WIDTH), lambda b, m: (b, C_NA_K // NA_WIDTH)),
            pl.BlockSpec((seq, NA_WIDTH), lambda b, m: (b, C_NA_V // NA_WIDTH)),
            pl.BlockSpec((1, NA_HEADS, tq, NA_KROWS * GRID_W), lambda b, m: (variant(m), 0, 0, 0)),
        ],
        out_specs=pl.BlockSpec((tq, NA_WIDTH), lambda b, m: (b * nblk + m, 0)),
        out_shape=jax.ShapeDtypeStruct((batch * seq, NA_WIDTH), BF16),
        compiler_params=_params(("parallel", "arbitrary")),
        name="na_attention",
    )(proj, proj, proj, bias)


def _na_bias_tables(rpb, rows):
    nblk = rows // NA_ROWS
    qr = np.arange(NA_ROWS)
    kk = np.arange(NA_KROWS)
    plan = []
    for m in (0, 1, nblk - 1):
        r = NA_ROWS * m + qr
        r0 = np.clip(r - NA_WIN_ROWS // 2, 0, rows - NA_WIN_ROWS)
        kr0 = np.clip(NA_ROWS * m - NA_ROWS, 0, rows - NA_KROWS)
        key_row = kr0 + kk
        ok = (key_row[None, :] >= r0[:, None]) & (key_row[None, :] < r0[:, None] + NA_WIN_ROWS)
        a = key_row[None, :] - r[:, None] + NA_WIN_ROWS - 1
        plan.append(np.where(ok, a, -1).tolist())
    c = np.arange(GRID_W)
    cs = np.clip(c - NA_WIN_COLS // 2, 0, GRID_W - NA_WIN_COLS)
    col_ok = (c[None, :] >= cs[:, None]) & (c[None, :] < cs[:, None] + NA_WIN_COLS)
    bc = np.clip(c[None, :] - c[:, None] + NA_WIN_COLS - 1, 0, 2 * NA_WIN_COLS - 2)
    col_sel = np.zeros((GRID_W, GRID_W, 2 * NA_WIN_COLS - 1), np.float32)
    col_sel[c[:, None], c[None, :], bc] = 1.0
    blocks = jnp.einsum("hab,cjb->hacj", rpb.astype(F32), jnp.asarray(col_sel), precision=lax.Precision.HIGHEST)
    blocks = jnp.where(jnp.asarray(col_ok)[None, None], blocks, NEG_BIG)
    nrow = 2 * NA_WIN_ROWS - 1
    return pl.pallas_call(
        functools.partial(_na_bias_kernel, plan=plan),
        grid=(NA_HEADS,),
        in_specs=[pl.BlockSpec((1, nrow, GRID_W, GRID_W), lambda h: (h, 0, 0, 0))],
        out_specs=pl.BlockSpec((3, 1, NA_ROWS * GRID_W, NA_KROWS * GRID_W), lambda h: (0, h, 0, 0)),
        out_shape=jax.ShapeDtypeStruct((3, NA_HEADS, NA_ROWS * GRID_W, NA_KROWS * GRID_W), F32),
        compiler_params=_params(("parallel",)),
        name="na_bias_table",
    )(blocks)


def _na_bias_kernel(t_ref, o_ref, *, plan):
    neg = jnp.full((GRID_W, GRID_W), NEG_BIG, F32)
    for v, rows_plan in enumerate(plan):
        for qr, keys in enumerate(rows_plan):
            for kk in range(0, NA_KROWS, 2):
                pair = [t_ref[0, a] if a >= 0 else neg for a in keys[kk:kk + 2]]
                o_ref[v, 0, qr * GRID_W:(qr + 1) * GRID_W, kk * GRID_W:(kk + 2) * GRID_W] = (
                    jnp.concatenate(pair, axis=1))


def _mla_prep_kernel(ql_ref, kvl_ref, kr_ref, qn_ref, kvn_ref, wq_ref, wqr_ref, wk_ref, wv_ref,
                     ones_ref, cos_ref, sin_ref, q_out, k_out, v_out):
    cos = cos_ref[...]
    sin = sin_ref[...]
    cos8 = jnp.concatenate([cos] * MLA_HEADS, axis=1)
    sin8 = jnp.concatenate([sin] * MLA_HEADS, axis=1)
    hq = _rms(ql_ref[...].astype(F32), qn_ref[...], MLA_Q_RANK).astype(BF16)
    q = _dot(hq, wq_ref[...]) * cos8 + _dot(hq, wqr_ref[...]) * sin8
    q_out[...] = (q * MLA_QSCALE).astype(BF16)
    hkv = _rms(kvl_ref[...].astype(F32), kvn_ref[...]).astype(BF16)
    kr = kr_ref[...].astype(F32)
    kpe = kr[:, :HEAD_SLOT] * cos + kr[:, HEAD_SLOT:] * sin
    k = _dot(hkv, wk_ref[...]) + jnp.concatenate([kpe] * MLA_HEADS, axis=1)
    k_out[0] = k.T.astype(BF16)
    v_out[...] = (_dot(hkv, wv_ref[...]) + ones_ref[...]).astype(BF16)


def _mla_prep(proj, qn, kvn, wq, wqr, wk, wv, ones, cos_t, sin_t, seq):
    t = proj.shape[0]
    width = MLA_HEADS * HEAD_SLOT
    sblk = seq // TM
    full = lambda shape: pl.BlockSpec(shape, lambda i: (0, 0))
    out = jax.ShapeDtypeStruct((t, width), BF16)
    return pl.pallas_call(
        _mla_prep_kernel,
        grid=(t // TM,),
        in_specs=[
            pl.BlockSpec((TM, Q_LAT_PAD), lambda i: (i, C_Q_LAT // Q_LAT_PAD)),
            pl.BlockSpec((TM, MLA_KV_RANK), lambda i: (i, C_KV_LAT // MLA_KV_RANK)),
            pl.BlockSpec((TM, 2 * HEAD_SLOT), lambda i: (i, C_K_ROPE // (2 * HEAD_SLOT))),
            full((1, Q_LAT_PAD)), full((1, MLA_KV_RANK)),
            full((Q_LAT_PAD, width)), full((Q_LAT_PAD, width)),
            full((MLA_KV_RANK, width)), full((MLA_KV_RANK, width)),
            full((1, width)),
            pl.BlockSpec((TM, HEAD_SLOT), lambda i: (i % sblk, 0)),
            pl.BlockSpec((TM, HEAD_SLOT), lambda i: (i % sblk, 0)),
        ],
        out_specs=[pl.BlockSpec((TM, width), lambda i: (i, 0)),
                   pl.BlockSpec((1, width, TM), lambda i: (i // sblk, 0, i % sblk)),
                   pl.BlockSpec((TM, width), lambda i: (i, 0))],
        out_shape=[out, jax.ShapeDtypeStruct((t // seq, width, seq), BF16), out],
        compiler_params=_params(("parallel",)),
        name="mla_prep",
    )(proj, proj, proj, qn, kvn, wq, wqr, wk, wv, ones, cos_t, sin_t)


def _flash_kernel(q_ref, kt_ref, v_ref, o_ref):
    seq = v_ref.shape[0]
    nsub = q_ref.shape[0] // FLASH_TQ
    hslices = [slice(h * HEAD_SLOT, (h + 1) * HEAD_SLOT) for h in range(2)]
    streams = [(qi, h) for qi in range(nsub) for h in range(2)]
    qs = {(qi, h): q_ref[qi * FLASH_TQ:(qi + 1) * FLASH_TQ, hslices[h]] for qi, h in streams}
    ms = {st: jnp.full((FLASH_TQ, 1), NEG_BIG, F32) for st in streams}
    accs = {st: jnp.zeros((FLASH_TQ, HEAD_SLOT), F32) for st in streams}
    items = [(j, st) for j in range(seq // FLASH_TK) for st in streams]

    def scores(item):
        j, st = item
        return _dot(qs[st], kt_ref[0, hslices[st[1]], j * FLASH_TK:(j + 1) * FLASH_TK])

    s_next = scores(items[0])
    for n, (j, st) in enumerate(items):
        s = s_next
        if n + 1 < len(items):
            s_next = scores(items[n + 1])
        v = v_ref[j * FLASH_TK:(j + 1) * FLASH_TK, hslices[st[1]]]
        m_new = jnp.maximum(ms[st], jnp.max(s, axis=1, keepdims=True))
        p = jnp.exp2((s - m_new).astype(BF16))
        alpha = jnp.exp2(ms[st] - m_new)
        accs[st] = alpha * accs[st] + _dot(p, v)
        ms[st] = m_new
    lane = lax.broadcasted_iota(jnp.int32, (FLASH_TQ, HEAD_SLOT), 1)
    for qi in range(nsub):
        o0, o1 = [accs[(qi, h)] / pltpu.roll(accs[(qi, h)], MLA_V, axis=1) for h in range(2)]
        o_ref[qi * FLASH_TQ:(qi + 1) * FLASH_TQ, :] = jnp.where(lane < MLA_V, o0, o1).astype(BF16)


def _flash(qp, kp, vp, batch, seq):
    tq = FLASH_TQ * FLASH_SUBTILES
    nq = seq // tq
    return pl.pallas_call(
        _flash_kernel,
        grid=(batch, MLA_HEADS // 2, nq),
        in_specs=[
            pl.BlockSpec((tq, 2 * HEAD_SLOT), lambda b, hp, i: (b * nq + i, hp)),
            pl.BlockSpec((1, 2 * HEAD_SLOT, seq), lambda b, hp, i: (b, hp, 0)),
            pl.BlockSpec((seq, 2 * HEAD_SLOT), lambda b, hp, i: (b, hp)),
        ],
        out_specs=pl.BlockSpec((tq, LANES), lambda b, hp, i: (b * nq + i, hp)),
        out_shape=jax.ShapeDtypeStruct((batch * seq, MLA_WIDTH), BF16),
        compiler_params=_params(("parallel", "parallel", "arbitrary")),
        name="mla_flash",
    )(qp, kp, vp)


def _to_slab(o_ref, x):
    for j in range(D_MODEL // LANES):
        o_ref[pl.ds(j, x.shape[0], stride=D_MODEL // LANES), :] = x[:, j * LANES:(j + 1) * LANES]


def _from_slab(x_ref, rows):
    n = D_MODEL // LANES
    return jnp.concatenate([x_ref[pl.ds(j, rows, stride=n), :] for j in range(n)], axis=1)


def _merge_kernel(x_ref, ya_ref, yb_ref, ga_ref, gb_ref, wa_ref, wb_ref, wo_ref, gm_ref, wr_ref, o_ref, aff_ref):
    ya = _dot(ya_ref[...], wa_ref[...])
    yb = _dot(yb_ref[...], wb_ref[...])
    merged = _sigmoid(ga_ref[...].astype(F32)) * ya + _sigmoid(gb_ref[...].astype(F32)) * yb
    x = x_ref[...] + _dot(merged.astype(BF16), wo_ref[...])
    _to_slab(o_ref, x)
    h = _rms(x, gm_ref[...])
    h_hi = h.astype(BF16)
    h_lo = (h - h_hi.astype(F32)).astype(BF16)
    r = _dot(h_hi, wr_ref[...])
    logits = (r[:, :LANES] + r[:, LANES:]) + _dot(h_lo, wr_ref[:, :LANES])
    lane = lax.broadcasted_iota(jnp.int32, logits.shape, 1)
    logits = jnp.where(lane < N_EXPERTS, logits, NEG_BIG)
    e = jnp.exp(logits - jnp.max(logits, axis=1, keepdims=True))
    aff = e / jnp.sum(e, axis=1, keepdims=True)
    aff_ref[0] = aff.T[:N_EXPERTS, :]


def _merge(x2, ya, yb, proj, wa, wb, wo, gm, wr, seq):
    t = x2.shape[0]
    sblk = seq // TM
    nslab = D_MODEL // LANES
    full = lambda shape: pl.BlockSpec(shape, lambda i: (0, 0))
    return pl.pallas_call(
        _merge_kernel,
        grid=(t // TM,),
        in_specs=[
            pl.BlockSpec((TM, D_MODEL), lambda i: (i, 0)),
            pl.BlockSpec((TM, NA_WIDTH), lambda i: (i, 0)),
            pl.BlockSpec((TM, MLA_WIDTH), lambda i: (i, 0)),
            pl.BlockSpec((TM, D_MODEL), lambda i: (i, C_GATE_A // D_MODEL)),
            pl.BlockSpec((TM, D_MODEL), lambda i: (i, C_GATE_B // D_MODEL)),
            full((NA_WIDTH, D_MODEL)), full((MLA_WIDTH, D_MODEL)), full((D_MODEL, D_MODEL)),
            full((1, D_MODEL)), full((D_MODEL, 2 * LANES)),
        ],
        out_specs=[pl.BlockSpec((TM * nslab, LANES), lambda i: (i, 0)),
                   pl.BlockSpec((1, N_EXPERTS, TM), lambda i: (i // sblk, 0, i % sblk))],
        out_shape=[jax.ShapeDtypeStruct((t * nslab, LANES), F32),
                   jax.ShapeDtypeStruct((t // seq, N_EXPERTS, seq), F32)],
        compiler_params=_params(("parallel",)),
        name="merge",
    )(x2, ya, yb, proj, proj, wa, wb, wo, gm, wr)


def _prep_router(w):
    hi = w.astype(BF16)
    lo = (w - hi.astype(F32)).astype(BF16)
    z = jnp.zeros((D_MODEL, LANES - N_EXPERTS), BF16)
    return jnp.concatenate([hi, z, lo, z], axis=1)


def _cumsum_lanes(x):
    n = x.shape[1]
    lane = lax.broadcasted_iota(jnp.int32, x.shape, 1)
    sh = 1
    while sh < n:
        x = x + jnp.where(lane >= sh, pltpu.roll(x, sh, axis=1), 0.0)
        sh *= 2
    return x


def _select_kernel(aff_ref, val_ref, idx_ref, *, cap):
    aff = aff_ref[0]
    seq = aff.shape[1]
    bits = pltpu.bitcast(aff, jnp.int32)
    thr = jnp.zeros((N_EXPERTS, 1), jnp.int32)
    for bit in range(30, -1, -1):
        cand = thr | (1 << bit)
        cnt = jnp.sum((bits >= cand).astype(jnp.int32), axis=1, keepdims=True)
        thr = jnp.where(cnt >= cap, cand, thr)
    gt = bits > thr
    eq = bits == thr
    need = cap - jnp.sum(gt.astype(F32), axis=1, keepdims=True)
    sel = gt | (eq & (_cumsum_lanes(eq.astype(F32)) <= need))
    slot = jnp.where(sel, _cumsum_lanes(sel.astype(F32)) - 1.0, -1.0)

    hi = aff.astype(BF16).astype(F32)
    r1 = aff - hi
    mid = r1.astype(BF16).astype(F32)
    lo = (r1 - mid).astype(BF16).astype(F32)
    tok = lax.broadcasted_iota(jnp.int32, (1, seq), 1)
    t_hi = (tok >> 6).astype(F32)
    t_lo = (tok & 63).astype(F32)
    pad = jnp.zeros((SELECT_COLS - 5, seq), F32)
    cblk = 256
    for e in range(N_EXPERTS):
        rhs = jnp.concatenate([hi[e:e + 1], mid[e:e + 1], lo[e:e + 1], t_hi, t_lo, pad], axis=0).astype(BF16)
        slot_e = slot[e:e + 1, :]
        ci = lax.broadcasted_iota(jnp.int32, (cblk, 1), 0).astype(F32).astype(BF16)
        for cc in range(cap // cblk):
            rel = (slot_e - float(cc * cblk)).astype(BF16)
            hit = jnp.broadcast_to(rel, (cblk, seq)) == jnp.broadcast_to(ci, (cblk, seq))
            onehot = jnp.where(hit, jnp.ones((cblk, seq), BF16), jnp.zeros((cblk, seq), BF16))
            r = _dot_nt(onehot, rhs)
            cs = slice(cc * cblk, (cc + 1) * cblk)
            val_ref[0, e, cs, :] = (r[:, 0:1] + r[:, 1:2]) + r[:, 2:3]
            rt = r.T
            idx_ref[0, e, :, cs] = (rt[3:4, :] * 64.0 + rt[4:5, :]).astype(jnp.int32)


def _select(aff_t, cap):
    batch, _, seq = aff_t.shape
    return pl.pallas_call(
        functools.partial(_select_kernel, cap=cap),
        grid=(batch,),
        in_specs=[pl.BlockSpec((1, N_EXPERTS, seq), lambda b: (b, 0, 0))],
        out_specs=[pl.BlockSpec((1, N_EXPERTS, cap, 1), lambda b: (b, 0, 0, 0)),
                   pl.BlockSpec((1, N_EXPERTS, 1, cap), lambda b: (b, 0, 0, 0))],
        out_shape=[jax.ShapeDtypeStruct((batch, N_EXPERTS, cap, 1), F32),
                   jax.ShapeDtypeStruct((batch, N_EXPERTS, 1, cap), jnp.int32)],
        compiler_params=_params(("parallel",)),
        name="moe_select",
    )(aff_t)


def _gather_tokens(idx_ref, xs_ref, tile_ref, tokens):
    n = D_MODEL // LANES
    for c in tokens:
        row = pl.multiple_of(idx_ref[0, 0, c] * SUBLANES, SUBLANES)
        tile_ref[pl.ds(c, n, stride=TILE_PITCH), :] = xs_ref[pl.ds(row, SUBLANES), :]


def _scatter_tokens(idx_ref, tile_ref, acc_ref, tokens):
    n = D_MODEL // LANES
    rows, new = [], []
    for c in tokens:
        row = pl.multiple_of(idx_ref[0, 0, c] * SUBLANES, SUBLANES)
        rows.append(row)
        new.append(acc_ref[pl.ds(row, SUBLANES), :] + tile_ref[pl.ds(c, n, stride=TILE_PITCH), :])
    for row, val in zip(rows, new):
        acc_ref[pl.ds(row, SUBLANES), :] = val


def _tile_to_rows(tile_ref, cap):
    n = D_MODEL // LANES
    return jnp.concatenate([tile_ref[pl.ds(j * TILE_PITCH, cap), :] for j in range(n)], axis=1)


def _expert_ffn(xg_ref, w1_ref, w3_ref, w2_ref, side_work, finish):
    nsub = EXPERT_FF // MOE_SUB
    acts = []
    for k in range(nsub):
        cs = slice(k * MOE_SUB, (k + 1) * MOE_SUB)
        xg = xg_ref[...]
        h1 = _dot(xg, w1_ref[0, :, cs])
        h3 = _dot(xg, w3_ref[0, :, cs])
        acts.append(((h1 * _sigmoid(h1)) * h3).astype(BF16))
        side_work(k, 2 * nsub)
    act = jnp.concatenate(acts, axis=1)
    outs = []
    nout = D_MODEL // MOE_SUB
    for n in range(nout):
        outs.append(finish(n, _dot(act, w2_ref[0, :, n * MOE_SUB:(n + 1) * MOE_SUB])))
        side_work(nsub + n, 2 * nsub)
    return outs


def _moe_kernel(idxp_ref, idxc_ref, idxn_ref, x_hbm, val_ref, g_ref, w1_ref, w3_ref, w2_ref, o_hbm,
                xs_ref, acc_ref, xtile_ref, ytile_ref, xg_ref, sem, *, cap):
    b = pl.program_id(0)
    e = pl.program_id(1)
    last_e = pl.num_programs(1) - 1
    nchunk = D_MODEL // LANES
    groups = cap // GATHER_UNROLL

    @pl.when(e == 0)
    def _start_batch_row():
        c0 = pltpu.make_async_copy(x_hbm.at[b], xs_ref, sem.at[0])
        c1 = pltpu.make_async_copy(x_hbm.at[b], acc_ref, sem.at[1])
        c0.start()
        c1.start()
        ytile_ref[...] = jnp.zeros_like(ytile_ref)
        c0.wait()
        c1.wait()

        def body(i, carry):
            _gather_tokens(idxc_ref, xs_ref, xtile_ref, [i * GATHER_UNROLL + u for u in range(GATHER_UNROLL)])
            return carry

        lax.fori_loop(0, groups, body, 0)
        xg_ref[0] = _rms(_tile_to_rows(xtile_ref, cap), g_ref[...]).astype(BF16)

    def side(k, nsub):
        half = nsub // 2
        if k < half:
            lo, hi = k * groups // half, (k + 1) * groups // half
            for grp in range(lo, hi):
                _scatter_tokens(idxp_ref, ytile_ref, acc_ref, range(grp * GATHER_UNROLL, (grp + 1) * GATHER_UNROLL))
        else:
            k -= half
            _gather_tokens(idxn_ref, xs_ref, xtile_ref, range(k * cap // half, (k + 1) * cap // half))

    def finish(n, block):
        y = block * val_ref[0]
        per = MOE_SUB // LANES
        for j in range(per):
            ytile_ref[pl.ds((n * per + j) * TILE_PITCH, cap), :] = y[:, j * LANES:(j + 1) * LANES]

    _expert_ffn(xg_ref.at[e & 1], w1_ref, w3_ref, w2_ref, side, finish)
    xg_ref[(e + 1) & 1] = _rms(_tile_to_rows(xtile_ref, cap), g_ref[...]).astype(BF16)

    @pl.when(e == last_e)
    def _finish_batch_row():
        def body(i, carry):
            _scatter_tokens(idxc_ref, ytile_ref, acc_ref, [i * GATHER_UNROLL + u for u in range(GATHER_UNROLL)])
            return carry

        lax.fori_loop(0, groups, body, 0)
        c = pltpu.make_async_copy(acc_ref, o_hbm.at[b], sem.at[0])
        c.start()
        c.wait()


def _moe(x_slab, idx, val, g, w1, w3, w2, cap):
    batch, srows, _ = x_slab.shape
    assert (EXPERT_FF // MOE_SUB) % 2 == 0 and cap % GATHER_UNROLL == 0
    last = N_EXPERTS - 1
    idx_spec = lambda shift: pl.BlockSpec(
        (1, 1, cap), lambda b, e: (b * N_EXPERTS + jnp.clip(e + shift, 0, last), 0, 0), memory_space=pltpu.SMEM)
    tile = pltpu.VMEM((D_MODEL // LANES * TILE_PITCH, LANES), F32)
    return pl.pallas_call(
        functools.partial(_moe_kernel, cap=cap),
        grid=(batch, N_EXPERTS),
        in_specs=[
            idx_spec(-1), idx_spec(0), idx_spec(1),
            pl.BlockSpec(memory_space=pl.ANY),
            pl.BlockSpec((1, cap, 1), lambda b, e: (b * N_EXPERTS + e, 0, 0)),
            pl.BlockSpec((1, D_MODEL), lambda b, e: (0, 0)),
            pl.BlockSpec((1, D_MODEL, EXPERT_FF), lambda b, e: (e, 0, 0)),
            pl.BlockSpec((1, D_MODEL, EXPERT_FF), lambda b, e: (e, 0, 0)),
            pl.BlockSpec((1, EXPERT_FF, D_MODEL), lambda b, e: (e, 0, 0)),
        ],
        out_specs=pl.BlockSpec(memory_space=pl.ANY),
        out_shape=jax.ShapeDtypeStruct(x_slab.shape, F32),
        scratch_shapes=[
            pltpu.VMEM((srows, LANES), F32),
            pltpu.VMEM((srows, LANES), F32),
            tile, tile,
            pltpu.VMEM((2, cap, D_MODEL), BF16),
            pltpu.SemaphoreType.DMA((2,)),
        ],
        compiler_params=_params(("arbitrary", "arbitrary"), MOE_VMEM_LIMIT),
        name="moe_ffn",
    )(idx, idx, idx, x_slab, val, g, w1, w3, w2)


def _ple_kernel(x_ref, p_ref, g_ref, wg_ref, wp_ref, gf_ref, o_ref, *, final):
    x = _from_slab(x_ref, TM)
    gate = _sigmoid(_dot(_rms(x, g_ref[...]).astype(BF16), wg_ref[...]))
    y = x + gate * _dot(p_ref[0].astype(BF16), wp_ref[...])
    if final:
        y = _rms(y, gf_ref[...])
    o_ref[...] = y


def _ple(x_slab, p3, layer, g, wg, wp, gf, final):
    nslab = D_MODEL // LANES
    t = x_slab.shape[0] // nslab
    full = lambda shape: pl.BlockSpec(shape, lambda i: (0, 0))
    return pl.pallas_call(
        functools.partial(_ple_kernel, final=final),
        grid=(t // TM,),
        in_specs=[
            pl.BlockSpec((TM * nslab, LANES), lambda i: (i, 0)),
            pl.BlockSpec((1, TM, PLE_DIM), lambda i: (layer, i, 0)),
            full((1, D_MODEL)), full((D_MODEL, D_MODEL)), full((PLE_DIM, D_MODEL)), full((1, D_MODEL)),
        ],
        out_specs=pl.BlockSpec((TM, D_MODEL), lambda i: (i, 0)),
        out_shape=jax.ShapeDtypeStruct((t, D_MODEL), F32),
        compiler_params=_params(("parallel",)),
        name="ple",
    )(x_slab, p3, g, wg, wp, gf)


def _cast_kernel(w_ref, o_ref):
    o_ref[...] = w_ref[...].astype(BF16)


def _expert_weights_bf16(w, layer):
    _, e, r, c = w.shape
    return pl.pallas_call(
        _cast_kernel,
        grid=(e,),
        in_specs=[pl.BlockSpec((1, 1, r, c), lambda i: (layer, i, 0, 0))],
        out_specs=pl.BlockSpec((1, 1, r, c), lambda i: (0, i, 0, 0)),
        out_shape=jax.ShapeDtypeStruct((1, e, r, c), BF16),
        compiler_params=_params(("parallel",)),
        name="expert_weight_cast",
    )(w)[0]
def _rot_cols(w):
    q = MLA_ROPE // 4
    return jnp.concatenate([-w[:, q:2 * q], w[:, :q], -w[:, 3 * q:], w[:, 2 * q:3 * q]], axis=1)


def _prep_w_in(w):
    cuts = np.cumsum([NA_WIDTH, NA_WIDTH, NA_WIDTH, MLA_Q_RANK, MLA_KV_RANK, MLA_ROPE, D_MODEL])
    na_q, na_k, na_v, q_lat, kv_lat, k_rope, gate_a, gate_b = jnp.split(w, [int(c) for c in cuts], axis=1)
    z = lambda n: jnp.zeros((D_MODEL, n), w.dtype)
    out = jnp.concatenate([
        gate_a, gate_b, na_q, na_k, na_v,
        q_lat, z(Q_LAT_PAD - MLA_Q_RANK),
        kv_lat,
        z(MLA_NOPE), k_rope, z(HEAD_SLOT - MLA_NOPE - MLA_ROPE),
        z(MLA_NOPE), _rot_cols(k_rope), z(HEAD_SLOT - MLA_NOPE - MLA_ROPE),
    ], axis=1)
    assert out.shape[1] == N_PROJ
    return out.astype(BF16)


def _prep_mla_weights(wq_up, wkv_up):
    dqk = MLA_NOPE + MLA_ROPE
    wq = wq_up.reshape(MLA_Q_RANK, MLA_HEADS, dqk)
    nope, pe = wq[..., :MLA_NOPE], wq[..., MLA_NOPE:]
    pe_rot = _rot_cols(pe.reshape(MLA_Q_RANK * MLA_HEADS, MLA_ROPE)).reshape(MLA_Q_RANK, MLA_HEADS, MLA_ROPE)
    zq = lambda n: jnp.zeros((MLA_Q_RANK, MLA_HEADS, n), wq_up.dtype)
    tail = HEAD_SLOT - dqk
    wq_pad = jnp.concatenate([nope, pe, zq(tail)], axis=2).reshape(MLA_Q_RANK, MLA_HEADS * HEAD_SLOT)
    wq_rot = jnp.concatenate([zq(MLA_NOPE), pe_rot, zq(tail)], axis=2).reshape(MLA_Q_RANK, MLA_HEADS * HEAD_SLOT)
    rowpad = ((0, Q_LAT_PAD - MLA_Q_RANK), (0, 0))
    wq_pad = jnp.pad(wq_pad, rowpad).astype(BF16)
    wq_rot = jnp.pad(wq_rot, rowpad).astype(BF16)

    wkv = wkv_up.reshape(MLA_KV_RANK, MLA_HEADS, MLA_NOPE + MLA_V)
    k_nope, v = wkv[..., :MLA_NOPE], wkv[..., MLA_NOPE:]
    zk = jnp.zeros((MLA_KV_RANK, MLA_HEADS, HEAD_SLOT - MLA_NOPE), wkv_up.dtype)
    wk = jnp.concatenate([k_nope, zk], axis=2).reshape(MLA_KV_RANK, MLA_HEADS * HEAD_SLOT).astype(BF16)
    zv = jnp.zeros_like(v)
    even = (jnp.arange(MLA_HEADS) % 2 == 0)[None, :, None]
    wv = jnp.concatenate([jnp.where(even, v, zv), jnp.where(even, zv, v)], axis=2)
    wv = wv.reshape(MLA_KV_RANK, MLA_HEADS * HEAD_SLOT).astype(BF16)
    lane = np.arange(MLA_HEADS * HEAD_SLOT)
    head_even = (lane // HEAD_SLOT) % 2 == 0
    upper = (lane % HEAD_SLOT) >= MLA_V
    ones = jnp.asarray(np.where(head_even == upper, 1.0, 0.0).astype(np.float32))[None, :]
    return wq_pad, wq_rot, wk, wv, ones


def _rope_tables(seq):
    t = np.arange(seq)
    half = MLA_ROPE // 2
    freqs = 1.0 / (ROPE_BASE ** (jnp.arange(0, half, 2, dtype=F32) / half))

    def tab(pos):
        ang = jnp.asarray(pos, F32)[:, None] * freqs[None, :]
        return jnp.concatenate([jnp.cos(ang)] * 2, axis=1), jnp.concatenate([jnp.sin(ang)] * 2, axis=1)

    cr, sr = tab(t // GRID_W)
    cc, sc = tab(t % GRID_W)
    tail = HEAD_SLOT - MLA_NOPE - MLA_ROPE
    cos_t = jnp.concatenate([jnp.ones((seq, MLA_NOPE), F32), cr, cc, jnp.zeros((seq, tail), F32)], axis=1)
    sin_t = jnp.concatenate([jnp.zeros((seq, MLA_NOPE), F32), sr, sc, jnp.zeros((seq, tail), F32)], axis=1)
    return cos_t, sin_t


def kernel(x, p, norm_mix, w_in, na_rpb, mla_q_norm, mla_wq_up, mla_kv_norm, mla_wkv_up, w_na_o, w_mla_o, w_out,
           norm_moe, w_router, moe_w1, moe_w3, moe_w2, norm_ple, ple_gate_w, ple_w, norm_final):
    batch, seq, d = x.shape
    depth = w_in.shape[0]
    assert d == D_MODEL and seq % (GRID_W * NA_ROWS) == 0 and seq % TM == 0
    t = batch * seq
    cap = EC_CAPACITY * seq // N_EXPERTS
    assert TILE_PITCH == cap + SUBLANES and seq <= 64 * 64
    rows = seq // GRID_W
    cos_t, sin_t = _rope_tables(seq)
    row = lambda v: v.reshape(1, -1).astype(F32)

    x2 = x.reshape(t, d)
    for i in range(depth):
        proj = _in_proj(x2, row(norm_mix[i]), _prep_w_in(w_in[i]))
        ya = _na_attention(proj, _na_bias_tables(na_rpb[i], rows), batch, seq)
        wq, wqr, wk, wv, ones = _prep_mla_weights(mla_wq_up[i], mla_wkv_up[i])
        qn = jnp.pad(row(mla_q_norm[i]), ((0, 0), (0, Q_LAT_PAD - MLA_Q_RANK)))
        qp, kp, vp = _mla_prep(proj, qn, row(mla_kv_norm[i]), wq, wqr, wk, wv, ones, cos_t, sin_t, seq)
        yb = _flash(qp, kp, vp, batch, seq)
        x_slab, aff_t = _merge(x2, ya, yb, proj, w_na_o[i].astype(BF16), w_mla_o[i].astype(BF16),
                               w_out[i].astype(BF16), row(norm_moe[i]), _prep_router(w_router[i]), seq)

        val, idx = _select(aff_t, cap)
        x_slab = _moe(x_slab.reshape(batch, seq * (d // LANES), LANES),
                      idx.reshape(batch * N_EXPERTS, 1, cap), val.reshape(batch * N_EXPERTS, cap, 1),
                      row(norm_moe[i]), _expert_weights_bf16(moe_w1, i), _expert_weights_bf16(moe_w3, i),
                      _expert_weights_bf16(moe_w2, i), cap)

        x2 = _ple(x_slab.reshape(t * (d // LANES), LANES), p.reshape(depth, t, PLE_DIM), i, row(norm_ple[i]),
                  ple_gate_w[i].astype(BF16), ple_w[i].astype(BF16), row(norm_final), final=(i == depth - 1))
    return x2.reshape(batch, seq, d)
```
